```python
import math
import jax, jax.numpy as jnp
from jax import lax
import numpy as np

D_MODEL = 1024
BATCH = 8
SEQ = 2048
DEPTH = 2
DEC_BATCH = 32
DEC_SEQ = 4
PAST_LEN = 16384
PAGE_SIZE = 128

N_MIXERS = 2
N_META = 16
EPS = 1e-6
NEG_INF = -1e30
ROPE_THETA = 10000.0
RET_HEADS = 4
RET_DK = D_MODEL // RET_HEADS
RET_DV = 2 * D_MODEL // RET_HEADS
RET_CHUNK = 128
DIF_HEADS = 8
DIF_HD = D_MODEL // (2 * DIF_HEADS)
DIF_LAYER = 1
DIF_LAMBDA_INIT = 0.8 - 0.6 * math.exp(-0.3 * DIF_LAYER)
Q_BLOCK = 128
D_FF = 2816
CONV_W = 3

kernel_name = "retnet_diffattn_convffn_hybrid_step"

F32 = jnp.float32


def _rmsnorm(x, g):
    xf = x.astype(F32)
    y = xf * lax.rsqrt(jnp.mean(xf * xf, axis=-1, keepdims=True) + EPS)
    return (y * g.astype(F32)).astype(x.dtype)


def _ret_log_decay():
    return jnp.log1p(-jnp.exp2(-5.0 - jnp.arange(RET_HEADS, dtype=F32)))


def _ret_rotate(x, pos):
    angle = 1.0 / (ROPE_THETA ** jnp.linspace(0.0, 1.0, RET_DK // 2, dtype=F32))
    angle = jnp.repeat(angle, 2)
    th = pos[:, None] * angle[None, :]
    cos = jnp.cos(th)[None, :, None, :]
    sin = jnp.sin(th)[None, :, None, :]
    rot = jnp.stack([-x[..., 1::2], x[..., 0::2]], axis=-1).reshape(x.shape)
    return x * cos + rot * sin


def _ret_project(h, pos, w_q, w_k, w_v):
    b, l, _ = h.shape
    q = (h @ w_q).astype(F32).reshape(b, l, RET_HEADS, RET_DK)
    k = (h @ w_k).astype(F32).reshape(b, l, RET_HEADS, RET_DK)
    v = (h @ w_v).astype(F32).reshape(b, l, RET_HEADS, RET_DV)
    return _ret_rotate(q, pos), _ret_rotate(k, pos) * (RET_DK ** -0.5), v


def _ret_chunk(s0, q, k, v):
    log_g = _ret_log_decay()
    l = q.shape[1]
    idx = jnp.arange(l, dtype=F32)
    dist = idx[:, None] - idx[None, :]
    decay = jnp.where(dist >= 0, jnp.exp(log_g[:, None, None] * jnp.maximum(dist, 0.0)), 0.0)
    scores = jnp.einsum('blhd,bmhd->bhlm', q, k) * decay
    o = jnp.einsum('bhlm,bmhe->blhe', scores, v)
    q_dec = jnp.exp(log_g[None, :] * (idx[:, None] + 1.0))[None, :, :, None]
    o = o + jnp.einsum('blhd,bhde->blhe', q * q_dec, s0)
    k_dec = jnp.exp(log_g[None, :] * (l - 1.0 - idx[:, None]))[None, :, :, None]
    s1 = jnp.exp(log_g * l)[None, :, None, None] * s0 + jnp.einsum('blhd,blhe->bhde', k * k_dec, v)
    return o, s1


def _ret_output(o, h, w_g, w_o):
    b, l = o.shape[:2]
    o = o * lax.rsqrt(jnp.mean(o * o, axis=-1, keepdims=True) + EPS)
    o = o.reshape(b, l, RET_HEADS * RET_DV)
    gate = jax.nn.silu((h @ w_g).astype(F32))
    return (gate * o).astype(h.dtype) @ w_o


def retention_prompt(h, pos, w_q, w_k, w_v, w_g, w_o):
    b, t, _ = h.shape
    q, k, v = _ret_project(h, pos, w_q, w_k, w_v)
    s0 = jnp.zeros((b, RET_HEADS, RET_DK, RET_DV), F32)
    o_meta, s = _ret_chunk(s0, q[:, :N_META], k[:, :N_META], v[:, :N_META])
    n_chunks = (t - N_META) // RET_CHUNK

    def split(a):
        return a[:, N_META:].reshape(b, n_chunks, RET_CHUNK, *a.shape[2:]).swapaxes(0, 1)

    def body(state, qkv):
        o_c, state = _ret_chunk(state, *qkv)
        return state, o_c

    s, o_r = lax.scan(body, s, (split(q), split(k), split(v)))
    o_r = o_r.swapaxes(0, 1).reshape(b, n_chunks * RET_CHUNK, RET_HEADS, RET_DV)
    o = jnp.concatenate([o_meta, o_r], axis=1)
    return _ret_output(o, h, w_g, w_o), s


def retention_sample(h, pos, s_past, w_q, w_k, w_v, w_g, w_o):
    q, k, v = _ret_project(h, pos, w_q, w_k, w_v)
    o, s = _ret_chunk(s_past.astype(F32), q, k, v)
    return _ret_output(o, h, w_g, w_o), s


def _rope(x, pos):
    half = DIF_HD // 2
    inv = ROPE_THETA ** (-jnp.arange(half, dtype=F32) / half)
    th = pos[:, None] * inv[None, :]
    cos = jnp.cos(th)[None, :, None, None, :]
    sin = jnp.sin(th)[None, :, None, None, :]
    x1, x2 = x[..., :half], x[..., half:]
    return jnp.concatenate([x1 * cos - x2 * sin, x1 * sin + x2 * cos], axis=-1)


def _dif_project(h, pos, w_q, w_k, w_v):
    b, l, _ = h.shape
    q = _rope((h @ w_q).astype(F32).reshape(b, l, DIF_HEADS, 2, DIF_HD), pos)
    k = _rope((h @ w_k).astype(F32).reshape(b, l, DIF_HEADS, 2, DIF_HD), pos)
    v = (h @ w_v).astype(F32).reshape(b, l, DIF_HEADS, 2 * DIF_HD)
    return q, k, v


def _dif_lambda(lq1, lk1, lq2, lk2):
    e1 = jnp.exp(jnp.sum(lq1.astype(F32) * lk1.astype(F32)))
    e2 = jnp.exp(jnp.sum(lq2.astype(F32) * lk2.astype(F32)))
    return e1 - e2 + DIF_LAMBDA_INIT


def _dif_output(o, subln_g, w_o, dtype):
    b, l = o.shape[:2]
    o = o * lax.rsqrt(jnp.mean(o * o, axis=-1, keepdims=True) + EPS) * subln_g.astype(F32) * (1.0 - DIF_LAMBDA_INIT)
    return o.reshape(b, l, DIF_HEADS * 2 * DIF_HD).astype(dtype) @ w_o


def dif_prompt(h, pos, lam, w_q, w_k, w_v, subln_g, w_o):
    b, t, _ = h.shape
    q, k, v = _dif_project(h, pos, w_q, w_k, w_v)
    n_blk = -(-t // Q_BLOCK)
    tp = n_blk * Q_BLOCK
    qb = jnp.pad(q, ((0, 0), (0, tp - t), (0, 0), (0, 0), (0, 0)))
    qb = qb.reshape(b, n_blk, Q_BLOCK, DIF_HEADS, 2, DIF_HD).swapaxes(0, 1)
    qpos = jnp.arange(tp).reshape(n_blk, Q_BLOCK)
    kpos = jnp.arange(t)
    scale = DIF_HD ** -0.5

    def block(args):
        qi, qp = args
        s = jnp.einsum('bqhcd,bkhcd->bhcqk', qi, k) * scale
        s = jnp.where(qp[:, None] >= kpos[None, :], s, NEG_INF)
        p = jax.nn.softmax(s, axis=-1)
        a = p[:, :, 0] - lam * p[:, :, 1]
        return jnp.einsum('bhqk,bkhe->bqhe', a, v)

    o = lax.map(block, (qb, qpos))
    o = o.swapaxes(0, 1).reshape(b, tp, DIF_HEADS, 2 * DIF_HD)[:, :t]
    return _dif_output(o, subln_g, w_o, h.dtype), k, v


def dif_sample(h, pos, lam, cache_k, cache_v, page_table, w_q, w_k, w_v, subln_g, w_o):
    b, ds, _ = h.shape
    q, k, v = _dif_project(h, pos, w_q, w_k, w_v)
    scale = DIF_HD ** -0.5

    def update(carry, s, vb):
        m, l, acc = carry
        m_new = jnp.maximum(m, jnp.max(s, axis=-1))
        corr = jnp.exp(m - m_new)
        p = jnp.exp(s - m_new[..., None])
        return (m_new, l * corr + jnp.sum(p, axis=-1),
                acc * corr[..., None] + jnp.einsum('bhcqk,bkhe->bhcqe', p, vb))

    def page_step(carry, pt):
        kp = cache_k[pt].astype(F32)
        vp = cache_v[pt].astype(F32)
        s = jnp.einsum('bqhcd,bkhcd->bhcqk', q, kp) * scale
        return update(carry, s, vp), None

    init = (jnp.full((b, DIF_HEADS, 2, ds), NEG_INF, F32),
            jnp.zeros((b, DIF_HEADS, 2, ds), F32),
            jnp.zeros((b, DIF_HEADS, 2, ds, 2 * DIF_HD), F32))
    carry, _ = lax.scan(page_step, init, page_table.T)
    causal = jnp.arange(ds)[:, None] >= jnp.arange(ds)[None, :]
    s_new = jnp.where(causal, jnp.einsum('bqhcd,bkhcd->bhcqk', q, k) * scale, NEG_INF)
    _, l, acc = update(carry, s_new, v)
    o = acc / l[..., None]
    a = (o[:, :, 0] - lam * o[:, :, 1]).transpose(0, 2, 1, 3)
    return _dif_output(a, subln_g, w_o, h.dtype), k, v


def conv_ffn(h, prev, w_in, conv_w, conv_b, w_out):
    l = h.shape[1]
    a, g = jnp.split(h @ w_in, 2, axis=-1)
    a_full = jnp.concatenate([prev.astype(a.dtype), a], axis=1)
    a_c = sum(conv_w[j] * a_full[:, j:j + l] for j in range(CONV_W)) + conv_b
    y = (jax.nn.silu(a_c.astype(F32)) * g.astype(F32)).astype(h.dtype) @ w_out
    return y, a_full[:, l:]


def setup_inputs(seed: int = 0) -> dict:
    key = jax.random.key(seed)
    ks = jax.random.split(key, 32)
    n_pages = PAST_LEN // PAGE_SIZE
    n_pool = (DEC_BATCH * n_pages * 5) // 4

    def nrm(k, shape, scale=1.0):
        return jax.random.normal(k, shape, F32) * scale

    page_table = jax.random.permutation(ks[5], n_pool)[:DEC_BATCH * n_pages].reshape(DEC_BATCH, n_pages).astype(jnp.int32)
    d = D_MODEL
    return {
        "x_prompt": nrm(ks[0], (BATCH, SEQ, d)),
        "x_sample": nrm(ks[1], (DEC_BATCH, DEC_SEQ, d)),
        "state_ret": nrm(ks[2], (DEC_BATCH, RET_HEADS, RET_DK, RET_DV)),
        "cache_k": nrm(ks[3], (n_pool, PAGE_SIZE, DIF_HEADS, 2, DIF_HD)),
        "cache_v": nrm(ks[4], (n_pool, PAGE_SIZE, DIF_HEADS, 2 * DIF_HD)),
        "page_table": page_table,
        "state_ffn_conv": nrm(ks[6], (DEPTH, DEC_BATCH, CONV_W - 1, D_FF)),
        "meta_tokens": nrm(ks[7], (N_META, d)),
        "norm1_g": 1.0 + nrm(ks[8], (DEPTH, d), 0.01),
        "norm2_g": 1.0 + nrm(ks[9], (DEPTH, d), 0.01),
        "final_g": 1.0 + nrm(ks[10], (d,), 0.01),
        "ret_w_q": nrm(ks[11], (d, RET_HEADS * RET_DK), d ** -0.5),
        "ret_w_k": nrm(ks[12], (d, RET_HEADS * RET_DK), d ** -0.5),
        "ret_w_v": nrm(ks[13], (d, RET_HEADS * RET_DV), d ** -0.5),
        "ret_w_g": nrm(ks[14], (d, RET_HEADS * RET_DV), d ** -0.5),
        "ret_w_o": nrm(ks[15], (RET_HEADS * RET_DV, d), (RET_HEADS * RET_DV) ** -0.5),
        "dif_w_q": nrm(ks[16], (d, DIF_HEADS * 2 * DIF_HD), d ** -0.5),
        "dif_w_k": nrm(ks[17], (d, DIF_HEADS * 2 * DIF_HD), d ** -0.5),
        "dif_w_v": nrm(ks[18], (d, DIF_HEADS * 2 * DIF_HD), d ** -0.5),
        "dif_lq1": nrm(ks[19], (DIF_HD,), 0.1),
        "dif_lk1": nrm(ks[20], (DIF_HD,), 0.1),
        "dif_lq2": nrm(ks[21], (DIF_HD,), 0.1),
        "dif_lk2": nrm(ks[22], (DIF_HD,), 0.1),
        "dif_subln_g": 1.0 + nrm(ks[23], (2 * DIF_HD,), 0.01),
        "dif_w_o": nrm(ks[24], (DIF_HEADS * 2 * DIF_HD, d), (DIF_HEADS * 2 * DIF_HD) ** -0.5),
        "ffn_w_in": nrm(ks[25], (DEPTH, d, 2 * D_FF), d ** -0.5),
        "ffn_conv_w": nrm(ks[26], (DEPTH, CONV_W, D_FF), CONV_W ** -0.5),
        "ffn_conv_b": nrm(ks[27], (DEPTH, D_FF), 0.01),
        "ffn_w_out": nrm(ks[28], (DEPTH, D_FF, d), D_FF ** -0.5),
    }


def reference(x_prompt, x_sample, state_ret, cache_k, cache_v, page_table, state_ffn_conv,
              meta_tokens, norm1_g, norm2_g, final_g,
              ret_w_q, ret_w_k, ret_w_v, ret_w_g, ret_w_o,
              dif_w_q, dif_w_k, dif_w_v, dif_lq1, dif_lk1, dif_lq2, dif_lk2, dif_subln_g, dif_w_o,
              ffn_w_in, ffn_conv_w, ffn_conv_b, ffn_w_out):
    b = x_prompt.shape[0]
    meta = jnp.broadcast_to(meta_tokens.astype(x_prompt.dtype)[None], (b, N_META, D_MODEL))
    hp = jnp.concatenate([meta, x_prompt], axis=1)
    hs = x_sample
    pos_p = jnp.arange(hp.shape[1], dtype=F32)
    pos_s = PAST_LEN + jnp.arange(hs.shape[1], dtype=F32)
    conv_p, conv_s = [], []
    for i in range(DEPTH):
        n_p = _rmsnorm(hp, norm1_g[i])
        n_s = _rmsnorm(hs, norm1_g[i])
        if i % N_MIXERS == 0:
            mp, ret_p = retention_prompt(n_p, pos_p, ret_w_q, ret_w_k, ret_w_v, ret_w_g, ret_w_o)
            ms, ret_s = retention_sample(n_s, pos_s, state_ret, ret_w_q, ret_w_k, ret_w_v, ret_w_g, ret_w_o)
        else:
            lam = _dif_lambda(dif_lq1, dif_lk1, dif_lq2, dif_lk2)
            mp, k_p, v_p = dif_prompt(n_p, pos_p, lam, dif_w_q, dif_w_k, dif_w_v, dif_subln_g, dif_w_o)
            ms, k_s, v_s = dif_sample(n_s, pos_s, lam, cache_k, cache_v, page_table,
                                      dif_w_q, dif_w_k, dif_w_v, dif_subln_g, dif_w_o)
        hp = hp + mp
        hs = hs + ms
        zeros_prev = jnp.zeros((b, CONV_W - 1, D_FF), hp.dtype)
        fp, cp = conv_ffn(_rmsnorm(hp, norm2_g[i]), zeros_prev, ffn_w_in[i], ffn_conv_w[i], ffn_conv_b[i], ffn_w_out[i])
        fs, cs = conv_ffn(_rmsnorm(hs, norm2_g[i]), state_ffn_conv[i], ffn_w_in[i], ffn_conv_w[i], ffn_conv_b[i], ffn_w_out[i])
        hp = hp + fp
        hs = hs + fs
        conv_p.append(cp)
        conv_s.append(cs)
    y_prompt = _rmsnorm(hp, final_g)[:, N_META:]
    y_sample = _rmsnorm(hs, final_g)
    ret_state_prompt = ret_p.astype(x_prompt.dtype)
    ret_state_sample = ret_s.astype(x_sample.dtype)
    k_rows_prompt = k_p.astype(x_prompt.dtype)
    v_rows_prompt = v_p.astype(x_prompt.dtype)
    k_rows_sample = k_s.astype(x_sample.dtype)
    v_rows_sample = v_s.astype(x_sample.dtype)
    conv_state_prompt = jnp.stack(conv_p)
    conv_state_sample = jnp.stack(conv_s)
    return (y_prompt, y_sample, ret_state_prompt, ret_state_sample, k_rows_prompt, v_rows_prompt,
            k_rows_sample, v_rows_sample, conv_state_prompt, conv_state_sample)
```

```python
import functools
import math

import jax
import jax.numpy as jnp
from jax import lax
from jax.experimental import pallas as pl
from jax.experimental.pallas import tpu as pltpu

F32 = jnp.float32
BF16 = jnp.bfloat16

N_META = 16
EPS = 1e-6
NEG_INF = -1e30
ROPE_THETA = 10000.0
RET_HEADS = 4
DIF_HEADS = 8
DIF_LAYER = 1
DIF_LAMBDA_INIT = 0.8 - 0.6 * math.exp(-0.3 * DIF_LAYER)
CONV_W = 3

V7X_LANES = 128
V7X_SUBLANES = 8
V7X_BF16_ROWS = 2 * V7X_SUBLANES
V7X_MXU_DIM = 256
V7X_VMEM_LIMIT_BYTES = 56 * 1024 * 1024

RET_CHUNK = V7X_MXU_DIM
ATT_BLOCK = V7X_MXU_DIM
FFN_COL_CHUNK = V7X_MXU_DIM
SAMPLE_PAD_T = V7X_BF16_ROWS
SAMPLE_WINDOW = V7X_LANES
DECODE_PAGES_PER_STEP = 8


def _params(*sem):
    return pltpu.CompilerParams(dimension_semantics=sem, vmem_limit_bytes=V7X_VMEM_LIMIT_BYTES)


def _const_spec(shape):
    nd = len(shape)
    return pl.BlockSpec(shape, lambda *_: (0,) * nd, pipeline_mode=pl.Buffered(1))


def _log2(n):
    assert n > 0 and n & (n - 1) == 0, n
    return n.bit_length() - 1


def _div_pow2(x, n):
    return lax.shift_right_logical(x, _log2(n))


def _mod_pow2(x, n):
    _log2(n)
    return lax.bitwise_and(x, n - 1)


def _rmsnorm(x, g):
    return x * lax.rsqrt(jnp.mean(x * x, axis=-1, keepdims=True) + EPS) * g


def _silu(x):
    return x * (1.0 / (1.0 + jnp.exp(-x)))


def _dot(a, b):
    return jnp.dot(a, b, preferred_element_type=F32)


def _dot_nt(a, b):
    return lax.dot_general(a, b, (((1,), (1,)), ((), ())), preferred_element_type=F32)


def _dot_tn(a, b):
    return lax.dot_general(a, b, (((0,), (0,)), ((), ())), preferred_element_type=F32)


def _ret_log_decay():
    return jnp.log1p(-jnp.exp2(-5.0 - jnp.arange(RET_HEADS, dtype=F32)))


def _ret_decay_pow(n):
    return [math.exp(n * math.log1p(-2.0 ** (-5.0 - h))) for h in range(RET_HEADS)]


def _ret_proj_kernel(x_ref, g_ref, cos_ref, sa_ref, sb_ref, wq_ref, wk_ref, wv_ref, wg_ref,
                     q_ref, k_ref, v_ref, gate_ref, *, dk):
    n = _rmsnorm(x_ref[...], g_ref[...]).astype(BF16)

    def rotate_into(y, out_ref):
        for c0 in range(0, y.shape[1], V7X_LANES):
            t0 = c0 % dk
            blk = y[:, c0:c0 + V7X_LANES]
            nxt = pltpu.roll(blk, V7X_LANES - 1, axis=1)
            prv = pltpu.roll(blk, 1, axis=1)
            out = (blk * cos_ref[:, t0:t0 + V7X_LANES] + nxt * sa_ref[:, t0:t0 + V7X_LANES]
                   + prv * sb_ref[:, t0:t0 + V7X_LANES])
            out_ref[:, c0:c0 + V7X_LANES] = out.astype(out_ref.dtype)

    rotate_into(_dot(n, wq_ref[...]), q_ref)
    rotate_into(_dot(n, wk_ref[...]) * (dk ** -0.5), k_ref)
    v_ref[...] = _dot(n, wv_ref[...]).astype(v_ref.dtype)
    gate_ref[...] = _silu(_dot(n, wg_ref[...])).astype(gate_ref.dtype)


def _ret_tables(pos, dk):
    angle = 1.0 / (ROPE_THETA ** jnp.linspace(0.0, 1.0, dk // 2, dtype=F32))
    angle = jnp.repeat(angle, 2)
    th = pos[:, None] * angle[None, :]
    cos, sin = jnp.cos(th), jnp.sin(th)
    even = (jnp.arange(dk) % 2 == 0)[None, :]
    return cos, jnp.where(even, -sin, 0.0), jnp.where(even, 0.0, sin)


def _ret_proj(x2d, g, tables, wq, wk, wv, wg, *, row_tile, rows_per_table):
    n, d = x2d.shape
    dk = wq.shape[1] // RET_HEADS
    n_tab = rows_per_table // row_tile
    row = lambda w: pl.BlockSpec((row_tile, w), lambda i: (i, 0))
    tab = pl.BlockSpec((row_tile, dk), lambda i: (i % n_tab, 0))
    return pl.pallas_call(
        functools.partial(_ret_proj_kernel, dk=dk),
        grid=(n // row_tile,),
        in_specs=[row(d), _const_spec((1, d)), tab, tab, tab,
                  _const_spec(wq.shape), _const_spec(wk.shape), _const_spec(wv.shape), _const_spec(wg.shape)],
        out_specs=[row(wq.shape[1]), row(wk.shape[1]), row(wv.shape[1]), row(wg.shape[1])],
        out_shape=[jax.ShapeDtypeStruct((n, w.shape[1]), BF16) for w in (wq, wk, wv, wg)],
        compiler_params=_params("arbitrary"),
        name="ret_proj",
    )(x2d, g, *tables, wq, wk, wv, wg)


def _retention_kernel(q_ref, k_ref, v_ref, gate_ref, dec_ref, qd_ref, kd_ref, og_ref, st_ref,
                      s_ref, qp_ref, kp_ref, vp_ref, gp_ref, *, chunk, tail):
    c = pl.program_id(1)
    last = pl.num_programs(1) - 1
    dk = q_ref.shape[2] // RET_HEADS
    dv = v_ref.shape[2] // RET_HEADS

    @pl.when(c == 0)
    def _():
        s_ref[...] = jnp.zeros(s_ref.shape, F32)

    def step(q2, k2, v2, g2, length):
        g_len = _ret_decay_pow(length)
        k_fix = _ret_decay_pow(length - chunk)
        for h in range(RET_HEADS):
            qh = q2[:, h * dk:(h + 1) * dk]
            kh = k2[:, h * dk:(h + 1) * dk]
            vh = v2[:, h * dv:(h + 1) * dv]
            s = _dot_nt(qh, kh) * dec_ref[h]
            o = _dot(s.astype(BF16), vh)
            state = s_ref[h]
            o = o + _dot(qh, state.astype(BF16)) * qd_ref[h]
            k_scaled = (kh.astype(F32) * kd_ref[h]).astype(BF16)
            upd = _dot_tn(k_scaled, vh)
            if length != chunk:
                upd = upd * k_fix[h]
            s_ref[h] = g_len[h] * state + upd
            o = o * lax.rsqrt(jnp.mean(o * o, axis=-1, keepdims=True) + EPS)
            gate = g2[:, h * dv:(h + 1) * dv].astype(F32)
            og_ref[0, :length, h * dv:(h + 1) * dv] = (gate * o)[:length].astype(og_ref.dtype)

    def full_step():
        step(q_ref[0], k_ref[0], v_ref[0], gate_ref[0], chunk)

    if tail:
        pl.when(c < last)(full_step)

        @pl.when(c == last)
        def _():
            for src, dst in ((q_ref, qp_ref), (k_ref, kp_ref), (v_ref, vp_ref), (gate_ref, gp_ref)):
                dst[...] = jnp.zeros(dst.shape, dst.dtype)
                dst[:tail, :] = src[0, :tail, :]
            step(qp_ref[...], kp_ref[...], vp_ref[...], gp_ref[...], tail)
    else:
        full_step()

    @pl.when(c == last)
    def _():
        st_ref[0] = s_ref[...]


def _retention(q, k, v, gate, *, chunk):
    b, t, _ = q.shape
    dk = q.shape[2] // RET_HEADS
    dv = v.shape[2] // RET_HEADS
    n_full, tail = divmod(t, chunk)
    assert tail % V7X_BF16_ROWS == 0
    n_chunks = n_full + (1 if tail else 0)
    log_g = _ret_log_decay()
    idx = jnp.arange(chunk, dtype=F32)
    dist = idx[:, None] - idx[None, :]
    decay = jnp.where(dist >= 0, jnp.exp(log_g[:, None, None] * jnp.maximum(dist, 0.0)), 0.0)
    q_dec = jnp.broadcast_to(jnp.exp(log_g[:, None] * (idx[None, :] + 1.0))[:, :, None], (RET_HEADS, chunk, dv))
    k_dec = jnp.broadcast_to(jnp.exp(log_g[:, None] * (chunk - 1.0 - idx[None, :]))[:, :, None],
                             (RET_HEADS, chunk, dk))
    seq = lambda w: pl.BlockSpec((1, chunk, w), lambda i, c: (i, c, 0))
    return pl.pallas_call(
        functools.partial(_retention_kernel, chunk=chunk, tail=tail),
        grid=(b, n_chunks),
        in_specs=[seq(RET_HEADS * dk), seq(RET_HEADS * dk), seq(RET_HEADS * dv), seq(RET_HEADS * dv),
                  _const_spec(decay.shape), _const_spec(q_dec.shape), _const_spec(k_dec.shape)],
        out_specs=[seq(RET_HEADS * dv), pl.BlockSpec((1, RET_HEADS, dk, dv), lambda i, c: (i, 0, 0, 0))],
        out_shape=[jax.ShapeDtypeStruct((b, t, RET_HEADS * dv), BF16),
                   jax.ShapeDtypeStruct((b, RET_HEADS, dk, dv), F32)],
        scratch_shapes=[pltpu.VMEM((RET_HEADS, dk, dv), F32),
                        pltpu.VMEM((chunk, RET_HEADS * dk), BF16), pltpu.VMEM((chunk, RET_HEADS * dk), BF16),
                        pltpu.VMEM((chunk, RET_HEADS * dv), BF16), pltpu.VMEM((chunk, RET_HEADS * dv), BF16)],
        compiler_params=_params("arbitrary", "arbitrary"),
        name="retention",
    )(q, k, v, gate, decay, q_dec, k_dec)


def _retention_sample_kernel(q_ref, k_ref, v_ref, gate_ref, dec_ref, qd_ref, kd_ref, s0_ref, og_ref, st_ref,
                             *, tp, n_real):
    b = pl.program_id(0)
    dk = q_ref.shape[1] // RET_HEADS
    dv = v_ref.shape[1] // RET_HEADS
    win = SAMPLE_WINDOW
    per_win = win // tp
    w0 = pl.multiple_of(_div_pow2(b, per_win) * win, win)
    slot = _mod_pow2(b, per_win)
    r0 = pl.multiple_of(slot * tp, tp)
    q0 = pl.multiple_of(w0 + r0, tp)
    row = lax.broadcasted_iota(jnp.int32, (win, 1), 0)
    in_seq = _div_pow2(row, tp) == slot
    g_len = _ret_decay_pow(n_real)
    for h in range(RET_HEADS):
        qh = q_ref[pl.ds(q0, tp), h * dk:(h + 1) * dk]
        k_win = k_ref[pl.ds(w0, win), h * dk:(h + 1) * dk]
        v_win = v_ref[pl.ds(w0, win), h * dv:(h + 1) * dv]
        s = _dot_nt(qh, k_win) * dec_ref[h, pl.ds(r0, tp), :]
        o = _dot(s.astype(BF16), v_win)
        state = s0_ref[0, h]
        o = o + _dot(qh, state.astype(BF16)) * qd_ref[h]
        k_scaled = jnp.where(in_seq, k_win.astype(F32) * kd_ref[h], 0.0).astype(BF16)
        st_ref[0, h] = g_len[h] * state + _dot_tn(k_scaled, v_win)
        o = o * lax.rsqrt(jnp.mean(o * o, axis=-1, keepdims=True) + EPS)
        gate = gate_ref[pl.ds(q0, tp), h * dv:(h + 1) * dv].astype(F32)
        og_ref[:, h * dv:(h + 1) * dv] = (gate * o).astype(og_ref.dtype)


def _retention_sample(q, k, v, gate, s0, *, n_seq, tp, n_real):
    n = q.shape[0]
    dk = q.shape[1] // RET_HEADS
    dv = v.shape[1] // RET_HEADS
    win = SAMPLE_WINDOW
    assert n == n_seq * tp and n % win == 0 and win % tp == 0
    log_g = _ret_log_decay()
    r = jnp.arange(win)
    tok = (r % tp).astype(F32)
    same = (r[:, None] // tp) == (r[None, :] // tp)
    dist = tok[:, None] - tok[None, :]
    decay = jnp.where(same & (dist >= 0), jnp.exp(log_g[:, None, None] * jnp.maximum(dist, 0.0)), 0.0)
    q_dec = jnp.broadcast_to(jnp.exp(log_g[:, None] * (tok[None, :tp] + 1.0))[:, :, None], (RET_HEADS, tp, dv))
    k_dec = jnp.broadcast_to(jnp.exp(log_g[:, None] * (n_real - 1.0 - tok[None, :]))[:, :, None],
                             (RET_HEADS, win, dk))
    state_spec = pl.BlockSpec((1, RET_HEADS, dk, dv), lambda i: (i, 0, 0, 0))
    return pl.pallas_call(
        functools.partial(_retention_sample_kernel, tp=tp, n_real=n_real),
        grid=(n_seq,),
        in_specs=[_const_spec(q.shape), _const_spec(k.shape), _const_spec(v.shape), _const_spec(gate.shape),
                  _const_spec(decay.shape), _const_spec(q_dec.shape), _const_spec(k_dec.shape), state_spec],
        out_specs=[pl.BlockSpec((tp, RET_HEADS * dv), lambda i: (i, 0)), state_spec],
        out_shape=[jax.ShapeDtypeStruct((n, RET_HEADS * dv), BF16),
                   jax.ShapeDtypeStruct((n_seq, RET_HEADS, dk, dv), F32)],
        compiler_params=_params("arbitrary"),
        name="retention_sample",
    )(q, k, v, gate, decay, q_dec, k_dec, s0)


def _mix_ffn_kernel(*refs, seq_len, has_prev, final_norm, full_a):
    refs = list(refs)
    x_ref, og_ref, wo_ref, g2_ref, wa_ref, wg_ref, cw_ref, cb_ref, wout_ref = refs[:9]
    refs = refs[9:]
    p1_ref = p2_ref = fg_ref = None
    if has_prev:
        p1_ref, p2_ref = refs[:2]
        refs = refs[2:]
    if final_norm:
        fg_ref = refs[0]
        refs = refs[1:]
    out_ref, a_ref, carry_ref, acc_ref = refs
    i = pl.program_id(0)
    rows = x_ref.shape[0]
    d_ff = wa_ref.shape[1]
    keep = V7X_SUBLANES

    @pl.when(i == 0)
    def _():
        carry_ref[...] = jnp.zeros(carry_ref.shape, F32)

    h = x_ref[...] + _dot(og_ref[...], wo_ref[...])
    n = _rmsnorm(h, g2_ref[...]).astype(BF16)
    row = lax.broadcasted_iota(jnp.int32, (rows, 1), 0)
    if seq_len % rows == 0:
        t = (i % (seq_len // rows)) * rows + row
    else:
        t = _mod_pow2(row, seq_len)
    acc_ref[...] = h
    for c0 in range(0, d_ff, FFN_COL_CHUNK):
        cs = slice(c0, c0 + FFN_COL_CHUNK)
        a = _dot(n, wa_ref[:, cs])
        gt = _dot(n, wg_ref[:, cs])
        prev = carry_ref[:, cs]
        r1 = jnp.where(row == 0, prev[keep - 1:keep], pltpu.roll(a, 1, axis=0))
        r2 = jnp.where(row == 0, prev[keep - 2:keep - 1],
                       jnp.where(row == 1, prev[keep - 1:keep], pltpu.roll(a, 2, axis=0)))
        am1 = jnp.where(t >= 1, r1, p1_ref[:, cs] if has_prev else 0.0)
        am2 = jnp.where(t >= 2, r2, p2_ref[:, cs] if has_prev else 0.0)
        ac = cw_ref[0:1, cs] * am2 + cw_ref[1:2, cs] * am1 + cw_ref[2:3, cs] * a + cb_ref[:, cs]
        u = (_silu(ac) * gt).astype(BF16)
        acc_ref[...] += _dot(u, wout_ref[cs, :])
        carry_ref[:, cs] = a[rows - keep:, :]
        if full_a:
            a_ref[:, cs] = a
        else:
            a_ref[0, :, cs] = a[rows - keep:, :]
    out = acc_ref[...]
    if final_norm:
        out = _rmsnorm(out, fg_ref[...])
    out_ref[...] = out


def _mix_ffn(x2d, og, wo, g2, wa, wg, conv_w, conv_b, wout, *, row_tile, seq_len, prev=None, final_g=None,
             full_a=False):
    n, d = x2d.shape
    d_ff = wa.shape[1]
    assert d_ff % FFN_COL_CHUNK == 0
    assert seq_len % row_tile == 0 or (row_tile % seq_len == 0 and n == row_tile)
    n_tiles = n // row_tile
    row = lambda w: pl.BlockSpec((row_tile, w), lambda i: (i, 0))
    in_specs = [row(d), row(og.shape[1]), _const_spec(wo.shape), _const_spec((1, d)), _const_spec(wa.shape),
                _const_spec(wg.shape), _const_spec(conv_w.shape), _const_spec((1, d_ff)), _const_spec(wout.shape)]
    args = [x2d, og, wo, g2, wa, wg, conv_w, conv_b, wout]
    if prev is not None:
        in_specs += [row(d_ff), row(d_ff)]
        args += list(prev)
    if final_g is not None:
        in_specs.append(_const_spec((1, d)))
        args.append(final_g)
    if full_a:
        a_spec, a_shape = row(d_ff), (n, d_ff)
    else:
        a_spec = pl.BlockSpec((1, V7X_SUBLANES, d_ff), lambda i: (i, 0, 0))
        a_shape = (n_tiles, V7X_SUBLANES, d_ff)
    return pl.pallas_call(
        functools.partial(_mix_ffn_kernel, seq_len=seq_len, has_prev=prev is not None,
                          final_norm=final_g is not None, full_a=full_a),
        grid=(n_tiles,),
        in_specs=in_specs,
        out_specs=[row(d), a_spec],
        out_shape=[jax.ShapeDtypeStruct((n, d), F32), jax.ShapeDtypeStruct(a_shape, F32)],
        scratch_shapes=[pltpu.VMEM((V7X_SUBLANES, d_ff), F32), pltpu.VMEM((row_tile, d), F32)],
        compiler_params=_params("arbitrary"),
        name="mix_ffn",
    )(*args)


def _dif_proj_kernel(x_ref, g_ref, cos_ref, sin_ref, wq_ref, wk_ref, wv_ref,
                     q_ref, k32_ref, v32_ref, k16_ref, v16_ref, *, hd):
    n = _rmsnorm(x_ref[...], g_ref[...]).astype(BF16)
    lane = lax.broadcasted_iota(jnp.int32, (1, V7X_LANES), 1)
    first_half = _mod_pow2(lane, hd) < (hd // 2)
    cos, sin = cos_ref[...], sin_ref[...]

    def rope(y, scale, outs):
        for c0 in range(0, y.shape[1], V7X_LANES):
            blk = y[:, c0:c0 + V7X_LANES]
            other = jnp.where(first_half, pltpu.roll(blk, V7X_LANES - hd // 2, axis=1),
                              pltpu.roll(blk, hd // 2, axis=1))
            out = blk * cos + other * sin
            if scale != 1.0:
                out = out * scale
            for o in outs:
                o[:, c0:c0 + V7X_LANES] = out.astype(o.dtype)

    rope(_dot(n, wq_ref[...]), hd ** -0.5, (q_ref,))
    rope(_dot(n, wk_ref[...]), 1.0, (k32_ref, k16_ref))
    v = _dot(n, wv_ref[...])
    v32_ref[...] = v
    v16_ref[...] = v.astype(BF16)


def _dif_tables(pos, hd):
    half = hd // 2
    inv = ROPE_THETA ** (-jnp.arange(half, dtype=F32) / half)
    th = pos[:, None] * inv[None, :]
    cos, sin = jnp.cos(th), jnp.sin(th)
    reps = V7X_LANES // hd
    return (jnp.tile(jnp.concatenate([cos, cos], axis=1), (1, reps)),
            jnp.tile(jnp.concatenate([-sin, sin], axis=1), (1, reps)))


def _dif_proj(x2d, g, tables, wq, wk, wv, *, row_tile, rows_per_table):
    n, d = x2d.shape
    hd = wq.shape[1] // (2 * DIF_HEADS)
    n_tab = rows_per_table // row_tile
    w = wq.shape[1]
    row = lambda width: pl.BlockSpec((row_tile, width), lambda i: (i, 0))
    tab = pl.BlockSpec((row_tile, V7X_LANES), lambda i: (i % n_tab, 0))
    return pl.pallas_call(
        functools.partial(_dif_proj_kernel, hd=hd),
        grid=(n // row_tile,),
        in_specs=[row(d), _const_spec((1, d)), tab, tab,
                  _const_spec(wq.shape), _const_spec(wk.shape), _const_spec(wv.shape)],
        out_specs=[row(w)] * 5,
        out_shape=[jax.ShapeDtypeStruct((n, w), BF16), jax.ShapeDtypeStruct((n, w), F32),
                   jax.ShapeDtypeStruct((n, w), F32), jax.ShapeDtypeStruct((n, w), BF16),
                   jax.ShapeDtypeStruct((n, w), BF16)],
        compiler_params=_params("arbitrary"),
        name="dif_proj",
    )(x2d, g, *tables, wq, wk, wv)


def _dif_lambda(lq1, lk1, lq2, lk2):
    e1 = jnp.exp(jnp.sum(lq1 * lk1, axis=-1, keepdims=True))
    e2 = jnp.exp(jnp.sum(lq2 * lk2, axis=-1, keepdims=True))
    return e1 - e2 + DIF_LAMBDA_INIT


def _subln(o, g):
    return o * lax.rsqrt(jnp.mean(o * o, axis=-1, keepdims=True) + EPS) * g * (1.0 - DIF_LAMBDA_INIT)


def _dif_flash_kernel(q_ref, k_ref, v_ref, lq1_ref, lk1_ref, lq2_ref, lk2_ref, sg_ref, o_ref,
                      qq_ref, kp_ref, vp_ref, m_ref, l_ref, acc_ref, *, block, hd):
    t = q_ref.shape[1]
    lam = _dif_lambda(lq1_ref[...], lk1_ref[...], lq2_ref[...], lk2_ref[...])
    lane = lax.broadcasted_iota(jnp.int32, (1, 2 * hd), 1)
    ri = _mod_pow2(lax.broadcasted_iota(jnp.int32, (2 * block, block), 0), block)
    ci = lax.broadcasted_iota(jnp.int32, (2 * block, block), 1)
    causal = ri >= ci

    def update(kb, vb, mask):
        s = _dot_nt(qq_ref[...], kb)
        if mask is not None:
            s = jnp.where(mask, s, NEG_INF)
        m_prev = m_ref[...]
        m_new = jnp.maximum(m_prev, jnp.max(s, axis=-1, keepdims=True))
        corr = jnp.exp(m_prev - m_new)
        p = jnp.exp(s - m_new)
        l_ref[...] = l_ref[...] * corr + jnp.sum(p, axis=-1, keepdims=True)
        acc_ref[...] = acc_ref[...] * corr + _dot(p.astype(BF16), vb)
        m_ref[...] = m_new

    def below(j, carry):
        start = pl.multiple_of(j * block, block)
        update(k_ref[0, pl.ds(start, block), :], v_ref[0, pl.ds(start, block), :], None)
        return carry

    for q0 in range(0, t, block):
        rows = min(block, t - q0)
        q = q_ref[0, q0:q0 + rows, :]
        if rows < block:
            qq_ref[...] = jnp.zeros(qq_ref.shape, BF16)
            kp_ref[...] = jnp.zeros(kp_ref.shape, BF16)
            vp_ref[...] = jnp.zeros(vp_ref.shape, BF16)
            kp_ref[:rows, :] = k_ref[0, q0:q0 + rows, :]
            vp_ref[:rows, :] = v_ref[0, q0:q0 + rows, :]
        qq_ref[:rows, :] = jnp.where(lane < hd, q, jnp.zeros_like(q))
        qq_ref[block:block + rows, :] = jnp.where(lane >= hd, q, jnp.zeros_like(q))
        m_ref[...] = jnp.full(m_ref.shape, NEG_INF, F32)
        l_ref[...] = jnp.zeros(l_ref.shape, F32)
        acc_ref[...] = jnp.zeros(acc_ref.shape, F32)
        lax.fori_loop(0, q0 // block, below, 0)
        if rows < block:
            update(kp_ref[...], vp_ref[...], causal)
        else:
            update(k_ref[0, q0:q0 + block, :], v_ref[0, q0:q0 + block, :], causal)
        o = acc_ref[...] / l_ref[...]
        a = o[:block] - lam * o[block:]
        o_ref[0, q0:q0 + rows, :] = _subln(a, sg_ref[...])[:rows].astype(o_ref.dtype)


def _dif_flash(q, k, v, lams, subln_g, *, block):
    b, t, w = q.shape
    hw = w // DIF_HEADS
    hd = hw // 2
    assert (t % block) % V7X_BF16_ROWS == 0
    head = pl.BlockSpec((1, t, hw), lambda i, h: (i, 0, h))
    vec = _const_spec((1, hd))
    return pl.pallas_call(
        functools.partial(_dif_flash_kernel, block=block, hd=hd),
        grid=(b, DIF_HEADS),
        in_specs=[head, head, head, vec, vec, vec, vec, _const_spec((1, hw))],
        out_specs=head,
        out_shape=jax.ShapeDtypeStruct((b, t, w), BF16),
        scratch_shapes=[pltpu.VMEM((2 * block, hw), BF16), pltpu.VMEM((block, hw), BF16),
                        pltpu.VMEM((block, hw), BF16), pltpu.VMEM((2 * block, 1), F32),
                        pltpu.VMEM((2 * block, 1), F32), pltpu.VMEM((2 * block, hw), F32)],
        compiler_params=_params("arbitrary", "arbitrary"),
        name="dif_flash",
    )(q, k, v, *lams, subln_g)


def _dif_decode_kernel(*refs, pages, n_tok, hd):
    qt_ref = refs[1]
    k_refs = refs[2:2 + pages]
    v_refs = refs[2 + pages:2 + 2 * pages]
    (kn_ref, vn_ref, lq1_ref, lk1_ref, lq2_ref, lk2_ref, sg_ref, o_ref,
     qbd_ref, knp_ref, vnp_ref, m_ref, l_ref, acc_ref) = refs[2 + 2 * pages:]
    j = pl.program_id(1)
    w, cols = qt_ref.shape[1:]
    hw = 2 * hd

    @pl.when(j == 0)
    def _():
        chan = lax.broadcasted_iota(jnp.int32, (w, cols), 0)
        col = lax.broadcasted_iota(jnp.int32, (w, cols), 1)
        col_map = _mod_pow2(col, DIF_HEADS) * 2 + _div_pow2(col, n_tok * DIF_HEADS)
        qbd_ref[...] = jnp.where(_div_pow2(chan, hd) == col_map, qt_ref[0], 0.0).astype(BF16)
        m_ref[...] = jnp.full(m_ref.shape, NEG_INF, F32)
        l_ref[...] = jnp.zeros(l_ref.shape, F32)
        acc_ref[...] = jnp.zeros(acc_ref.shape, F32)

    def to_column(r):
        ri = lax.broadcasted_iota(jnp.int32, (cols, cols), 0)
        ci = lax.broadcasted_iota(jnp.int32, (cols, cols), 1)
        return jnp.sum(jnp.where(ri == ci, r, 0.0), axis=1, keepdims=True)

    def update(kb, vb, mask):
        s = _dot(kb, qbd_ref[...])
        if mask is not None:
            s = jnp.where(mask, s, NEG_INF)
        m_prev = m_ref[...]
        m_new = jnp.maximum(m_prev, jnp.max(s, axis=0, keepdims=True))
        corr = jnp.exp(m_prev - m_new)
        p = jnp.exp(s - m_new)
        l_ref[...] = l_ref[...] * corr + jnp.sum(p, axis=0, keepdims=True)
        acc_ref[...] = acc_ref[...] * to_column(corr) + _dot_tn(p.astype(BF16), vb)
        m_ref[...] = m_new

    update(jnp.concatenate([r[0].astype(BF16) for r in k_refs], axis=0),
           jnp.concatenate([r[0].astype(BF16) for r in v_refs], axis=0), None)

    @pl.when(j == pl.num_programs(1) - 1)
    def _():
        n_new = kn_ref.shape[1]
        knp_ref[...] = jnp.zeros(knp_ref.shape, BF16)
        vnp_ref[...] = jnp.zeros(vnp_ref.shape, BF16)
        knp_ref[:n_new, :] = kn_ref[0].astype(BF16)
        vnp_ref[:n_new, :] = vn_ref[0].astype(BF16)
        key = lax.broadcasted_iota(jnp.int32, (knp_ref.shape[0], cols), 0)
        col = lax.broadcasted_iota(jnp.int32, (knp_ref.shape[0], cols), 1)
        tok = _mod_pow2(_div_pow2(col, DIF_HEADS), n_tok)
        update(knp_ref[...], vnp_ref[...], key <= tok)
        lam = _dif_lambda(lq1_ref[...], lk1_ref[...], lq2_ref[...], lk2_ref[...])
        o = acc_ref[...] / to_column(l_ref[...])
        sub = lax.broadcasted_iota(jnp.int32, (DIF_HEADS, hw), 0)
        for tk in range(n_tok):
            maps = []
            for c in range(2):
                r0 = (c * n_tok + tk) * DIF_HEADS
                grp = o[r0:r0 + DIF_HEADS, :]
                picked = jnp.zeros((DIF_HEADS, hw), F32)
                for h in range(DIF_HEADS):
                    picked = jnp.where(sub == h, grp[:, h * hw:(h + 1) * hw], picked)
                maps.append(picked)
            o_ref[0, tk] = _subln(maps[0] - lam * maps[1], sg_ref[...])


def _dif_decode(qt, cache_k, cache_v, page_table, k_new, v_new, lams, subln_g, *, n_tok, pages):
    b, w, cols = qt.shape
    hw = w // DIF_HEADS
    hd = hw // 2
    page = cache_k.shape[1]
    n_pages = page_table.shape[1]
    assert n_pages % pages == 0 and cols % V7X_LANES == 0
    page_spec = lambda p: pl.BlockSpec((1, page, w), lambda i, j, pt: (pt[i, j * pages + p], 0, 0))
    per_seq = lambda shape: pl.BlockSpec((1,) + shape, lambda i, j, pt: (i,) + (0,) * len(shape))
    const = lambda shape: pl.BlockSpec(shape, lambda i, j, pt: (0,) * len(shape))
    grid_spec = pltpu.PrefetchScalarGridSpec(
        num_scalar_prefetch=1,
        grid=(b, n_pages // pages),
        in_specs=[per_seq((w, cols))] + [page_spec(p) for p in range(pages)] * 2
                 + [per_seq(k_new.shape[1:]), per_seq(v_new.shape[1:])]
                 + [const((1, hd))] * 4 + [const((1, hw))],
        out_specs=per_seq((n_tok, DIF_HEADS, hw)),
        scratch_shapes=[pltpu.VMEM((w, cols), BF16), pltpu.VMEM((V7X_LANES, w), BF16),
                        pltpu.VMEM((V7X_LANES, w), BF16), pltpu.VMEM((1, cols), F32),
                        pltpu.VMEM((1, cols), F32), pltpu.VMEM((cols, w), F32)],
    )
    return pl.pallas_call(
        functools.partial(_dif_decode_kernel, pages=pages, n_tok=n_tok, hd=hd),
        grid_spec=grid_spec,
        out_shape=jax.ShapeDtypeStruct((b, n_tok, DIF_HEADS, hw), F32),
        compiler_params=_params("arbitrary", "arbitrary"),
        name="dif_decode",
    )(page_table, qt, *([cache_k] * pages), *([cache_v] * pages), k_new, v_new, *lams, subln_g)


def _largest_tile(n, cap):
    return max(r for r in range(V7X_BF16_ROWS, cap + 1, V7X_BF16_ROWS) if n % r == 0)


def kernel(x_prompt, x_sample, state_ret, cache_k, cache_v, page_table, state_ffn_conv, meta_tokens, norm1_g,
           norm2_g, final_g, ret_w_q, ret_w_k, ret_w_v, ret_w_g, ret_w_o, dif_w_q, dif_w_k, dif_w_v, dif_lq1,
           dif_lk1, dif_lq2, dif_lk2, dif_subln_g, dif_w_o, ffn_w_in, ffn_conv_w, ffn_conv_b, ffn_w_out):
    b, seq, d = x_prompt.shape
    db, ds, _ = x_sample.shape
    t = seq + N_META
    tp = SAMPLE_PAD_T
    assert ds <= tp
    d_ff = ffn_w_out.shape[1]
    page = cache_k.shape[1]
    past_len = page_table.shape[1] * page
    w_dif = dif_w_q.shape[1]
    hd = w_dif // (2 * DIF_HEADS)
    dk = ret_w_q.shape[1] // RET_HEADS
    bf = lambda a: a.astype(BF16)
    row = lambda a: a.reshape(1, -1).astype(F32)
    seq3 = lambda a, n, length: a.reshape(n, length, a.shape[-1])

    meta = jnp.broadcast_to(meta_tokens.astype(x_prompt.dtype)[None], (b, N_META, d))
    hp = jnp.concatenate([meta, x_prompt], axis=1).reshape(b * t, d)
    hs = jnp.pad(x_sample, ((0, 0), (0, tp - ds), (0, 0))).reshape(db * tp, d)
    pos_p = jnp.arange(t, dtype=F32)
    pos_s = jnp.tile(past_len + jnp.arange(tp, dtype=F32), db)
    tile_p = _largest_tile(t, 1024)
    tile_s = db * tp
    lams = [row(v) for v in (dif_lq1, dif_lk1, dif_lq2, dif_lk2)]
    subln_g = row(dif_subln_g)

    ret_w = [bf(w) for w in (ret_w_q, ret_w_k, ret_w_v, ret_w_g)]
    g1 = row(norm1_g[0])
    qp, kp, vp, gp = _ret_proj(hp, g1, _ret_tables(pos_p, dk), *ret_w, row_tile=tile_p, rows_per_table=t)
    qs, ks, vs, gs = _ret_proj(hs, g1, _ret_tables(pos_s, dk), *ret_w, row_tile=tile_s, rows_per_table=tile_s)
    og_p, ret_state_prompt = _retention(seq3(qp, b, t), seq3(kp, b, t), seq3(vp, b, t), seq3(gp, b, t),
                                        chunk=RET_CHUNK)
    og_s, ret_state_sample = _retention_sample(qs, ks, vs, gs, state_ret.astype(F32), n_seq=db, tp=tp, n_real=ds)

    def ffn_weights(i):
        w_a, w_g = jnp.split(ffn_w_in[i], 2, axis=-1)
        return (row(norm2_g[i]), bf(w_a), bf(w_g), ffn_conv_w[i].astype(F32), row(ffn_conv_b[i]), bf(ffn_w_out[i]))

    def prev_rows(state):
        z = jnp.zeros((db, tp, d_ff), F32)
        p1 = z.at[:, 0].set(state[:, 1])
        p2 = z.at[:, 0].set(state[:, 0]).at[:, 1].set(state[:, 1])
        return p1.reshape(db * tp, d_ff), p2.reshape(db * tp, d_ff)

    def conv_state_prompt(a_tail):
        per_seq = a_tail.reshape(b, t // tile_p, V7X_SUBLANES, d_ff)
        return per_seq[:, -1, V7X_SUBLANES - (CONV_W - 1):, :]

    def conv_state_sample(a_full):
        return a_full.reshape(db, tp, d_ff)[:, ds - (CONV_W - 1):ds, :]

    fw0 = ffn_weights(0)
    hp, a_p0 = _mix_ffn(hp, og_p.reshape(b * t, -1), bf(ret_w_o), *fw0, row_tile=tile_p, seq_len=t)
    hs, a_s0 = _mix_ffn(hs, og_s, bf(ret_w_o), *fw0, row_tile=tile_s, seq_len=tp,
                        prev=prev_rows(state_ffn_conv[0].astype(F32)), full_a=True)

    dif_w = [bf(w) for w in (dif_w_q, dif_w_k, dif_w_v)]
    g1 = row(norm1_g[1])
    q16, k32, v32, k16, v16 = _dif_proj(hp, g1, _dif_tables(pos_p, hd), *dif_w, row_tile=tile_p, rows_per_table=t)
    og_p = _dif_flash(seq3(q16, b, t), seq3(k16, b, t), seq3(v16, b, t), lams, subln_g, block=ATT_BLOCK)
    k_rows_prompt = k32.reshape(b, t, DIF_HEADS, 2, hd)
    v_rows_prompt = v32.reshape(b, t, DIF_HEADS, 2 * hd)

    qs16, ks32, vs32, _, _ = _dif_proj(hs, g1, _dif_tables(pos_s, hd), *dif_w, row_tile=tile_s,
                                       rows_per_table=tile_s)
    ks32 = ks32.reshape(db, tp, w_dif)
    vs32 = vs32.reshape(db, tp, w_dif)
    q_real = qs16.reshape(db, tp, w_dif)[:, :ds].astype(F32)
    qt = jnp.repeat(jnp.swapaxes(q_real, 1, 2), DIF_HEADS, axis=2)
    qt = jnp.tile(qt, (1, 1, 2))
    qt = jnp.pad(qt, ((0, 0), (0, 0), (0, -qt.shape[2] % V7X_LANES)))
    og_s = _dif_decode(qt, cache_k.reshape(-1, page, w_dif), cache_v.reshape(-1, page, w_dif), page_table,
                       ks32, vs32, lams, subln_g, n_tok=ds, pages=DECODE_PAGES_PER_STEP)
    og_s = jnp.pad(og_s.reshape(db, ds, w_dif), ((0, 0), (0, tp - ds), (0, 0))).astype(BF16)
    k_rows_sample = ks32[:, :ds].reshape(db, ds, DIF_HEADS, 2, hd)
    v_rows_sample = vs32[:, :ds].reshape(db, ds, DIF_HEADS, 2 * hd)

    fw1 = ffn_weights(1)
    fg = row(final_g)
    yp, a_p1 = _mix_ffn(hp, og_p.reshape(b * t, -1), bf(dif_w_o), *fw1, row_tile=tile_p, seq_len=t, final_g=fg)
    ys, a_s1 = _mix_ffn(hs, og_s.reshape(db * tp, -1), bf(dif_w_o), *fw1, row_tile=tile_s, seq_len=tp,
                        prev=prev_rows(state_ffn_conv[1].astype(F32)), final_g=fg, full_a=True)

    y_prompt = yp.reshape(b, t, d)[:, N_META:]
    y_sample = ys.reshape(db, tp, d)[:, :ds]
    conv_p = jnp.stack([conv_state_prompt(a_p0), conv_state_prompt(a_p1)])
    conv_s = jnp.stack([conv_state_sample(a_s0), conv_state_sample(a_s1)])
    return (y_prompt, y_sample, ret_state_prompt, ret_state_sample, k_rows_prompt, v_rows_prompt,
            k_rows_sample, v_rows_sample, conv_p, conv_s)
```

```python
import functools
import math

import jax
import jax.numpy as jnp
from jax import lax
from jax.experimental import pallas as pl
from jax.experimental.pallas import tpu as pltpu

F32 = jnp.float32
BF16 = jnp.bfloat16

N_META = 16
EPS = 1e-6
NEG_INF = -1e30
ROPE_THETA = 10000.0
RET_HEADS = 4
DIF_HEADS = 8
DIF_LAYER = 1
DIF_LAMBDA_INIT = 0.8 - 0.6 * math.exp(-0.3 * DIF_LAYER)
CONV_W = 3

V7X_LANES = 128
V7X_SUBLANES = 8
V7X_BF16_ROWS = 2 * V7X_SUBLANES
V7X_MXU_DIM = 256
V7X_VMEM_LIMIT_BYTES = 56 * 1024 * 1024

RET_CHUNK = V7X_MXU_DIM
ATT_BLOCK = V7X_MXU_DIM
FFN_COL_CHUNK = V7X_MXU_DIM
SAMPLE_PAD_T = V7X_BF16_ROWS
SAMPLE_WINDOW = V7X_LANES
DECODE_PAGES_PER_STEP = 8
DECODE_SLOTS = V7X_SUBLANES


def _params(*sem):
    return pltpu.CompilerParams(dimension_semantics=sem, vmem_limit_bytes=V7X_VMEM_LIMIT_BYTES)


def _const_spec(shape):
    nd = len(shape)
    return pl.BlockSpec(shape, lambda *_: (0,) * nd, pipeline_mode=pl.Buffered(1))


def _log2(n):
    assert n > 0 and n & (n - 1) == 0, n
    return n.bit_length() - 1


def _div_pow2(x, n):
    return lax.shift_right_logical(x, _log2(n))


def _mod_pow2(x, n):
    _log2(n)
    return lax.bitwise_and(x, n - 1)


def _rmsnorm(x, g):
    return x * lax.rsqrt(jnp.mean(x * x, axis=-1, keepdims=True) + EPS) * g


def _silu(x):
    return x * (1.0 / (1.0 + jnp.exp(-x)))


def _dot(a, b):
    return jnp.dot(a, b, preferred_element_type=F32)


def _dot_nt(a, b):
    return lax.dot_general(a, b, (((1,), (1,)), ((), ())), preferred_element_type=F32)


def _dot_tn(a, b):
    return lax.dot_general(a, b, (((0,), (0,)), ((), ())), preferred_element_type=F32)


def _ret_log_decay():
    return jnp.log1p(-jnp.exp2(-5.0 - jnp.arange(RET_HEADS, dtype=F32)))


def _ret_decay_pow(n):
    return [math.exp(n * math.log1p(-2.0 ** (-5.0 - h))) for h in range(RET_HEADS)]


def _ret_proj_kernel(x_ref, g_ref, cos_ref, sa_ref, sb_ref, wq_ref, wk_ref, wv_ref, wg_ref,
                     q_ref, k_ref, v_ref, gate_ref, *, dk):
    n = _rmsnorm(x_ref[...], g_ref[...]).astype(BF16)

    def rotate_into(y, out_ref):
        for c0 in range(0, y.shape[1], V7X_LANES):
            t0 = c0 % dk
            blk = y[:, c0:c0 + V7X_LANES]
            nxt = pltpu.roll(blk, V7X_LANES - 1, axis=1)
            prv = pltpu.roll(blk, 1, axis=1)
            out = (blk * cos_ref[:, t0:t0 + V7X_LANES] + nxt * sa_ref[:, t0:t0 + V7X_LANES]
                   + prv * sb_ref[:, t0:t0 + V7X_LANES])
            out_ref[:, c0:c0 + V7X_LANES] = out.astype(out_ref.dtype)

    rotate_into(_dot(n, wq_ref[...]), q_ref)
    rotate_into(_dot(n, wk_ref[...]) * (dk ** -0.5), k_ref)
    v_ref[...] = _dot(n, wv_ref[...]).astype(v_ref.dtype)
    gate_ref[...] = _silu(_dot(n, wg_ref[...])).astype(gate_ref.dtype)


def _ret_tables(pos, dk):
    angle = 1.0 / (ROPE_THETA ** jnp.linspace(0.0, 1.0, dk // 2, dtype=F32))
    angle = jnp.repeat(angle, 2)
    th = pos[:, None] * angle[None, :]
    cos, sin = jnp.cos(th), jnp.sin(th)
    even = (jnp.arange(dk) % 2 == 0)[None, :]
    return cos, jnp.where(even, -sin, 0.0), jnp.where(even, 0.0, sin)


def _ret_proj(x2d, g, tables, wq, wk, wv, wg, *, row_tile, rows_per_table):
    n, d = x2d.shape
    dk = wq.shape[1] // RET_HEADS
    n_tab = rows_per_table // row_tile
    row = lambda w: pl.BlockSpec((row_tile, w), lambda i: (i, 0))
    tab = pl.BlockSpec((row_tile, dk), lambda i: (i % n_tab, 0))
    return pl.pallas_call(
        functools.partial(_ret_proj_kernel, dk=dk),
        grid=(n // row_tile,),
        in_specs=[row(d), _const_spec((1, d)), tab, tab, tab,
                  _const_spec(wq.shape), _const_spec(wk.shape), _const_spec(wv.shape), _const_spec(wg.shape)],
        out_specs=[row(wq.shape[1]), row(wk.shape[1]), row(wv.shape[1]), row(wg.shape[1])],
        out_shape=[jax.ShapeDtypeStruct((n, w.shape[1]), BF16) for w in (wq, wk, wv, wg)],
        compiler_params=_params("arbitrary"),
        name="ret_proj",
    )(x2d, g, *tables, wq, wk, wv, wg)


def _retention_kernel(q_ref, k_ref, v_ref, gate_ref, dec_ref, qd_ref, kd_ref, og_ref, st_ref,
                      s_ref, qp_ref, kp_ref, vp_ref, gp_ref, *, chunk, tail):
    c = pl.program_id(1)
    last = pl.num_programs(1) - 1
    dk = q_ref.shape[2] // RET_HEADS
    dv = v_ref.shape[2] // RET_HEADS

    @pl.when(c == 0)
    def _():
        s_ref[...] = jnp.zeros(s_ref.shape, F32)

    def step(q2, k2, v2, g2, length):
        g_len = _ret_decay_pow(length)
        k_fix = _ret_decay_pow(length - chunk)
        for h in range(RET_HEADS):
            qh = q2[:, h * dk:(h + 1) * dk]
            kh = k2[:, h * dk:(h + 1) * dk]
            vh = v2[:, h * dv:(h + 1) * dv]
            s = _dot_nt(qh, kh) * dec_ref[h]
            o = _dot(s.astype(BF16), vh)
            state = s_ref[h]
            o = o + _dot(qh, state.astype(BF16)) * qd_ref[h]
            k_scaled = (kh.astype(F32) * kd_ref[h]).astype(BF16)
            upd = _dot_tn(k_scaled, vh)
            if length != chunk:
                upd = upd * k_fix[h]
            s_ref[h] = g_len[h] * state + upd
            o = o * lax.rsqrt(jnp.mean(o * o, axis=-1, keepdims=True) + EPS)
            gate = g2[:, h * dv:(h + 1) * dv].astype(F32)
            og_ref[0, :length, h * dv:(h + 1) * dv] = (gate * o)[:length].astype(og_ref.dtype)

    def full_step():
        step(q_ref[0], k_ref[0], v_ref[0], gate_ref[0], chunk)

    if tail:
        pl.when(c < last)(full_step)

        @pl.when(c == last)
        def _():
            for src, dst in ((q_ref, qp_ref), (k_ref, kp_ref), (v_ref, vp_ref), (gate_ref, gp_ref)):
                dst[...] = jnp.zeros(dst.shape, dst.dtype)
                dst[:tail, :] = src[0, :tail, :]
            step(qp_ref[...], kp_ref[...], vp_ref[...], gp_ref[...], tail)
    else:
        full_step()

    @pl.when(c == last)
    def _():
        st_ref[0] = s_ref[...]


def _retention(q, k, v, gate, *, chunk):
    b, t, _ = q.shape
    dk = q.shape[2] // RET_HEADS
    dv = v.shape[2] // RET_HEADS
    n_full, tail = divmod(t, chunk)
    assert tail % V7X_BF16_ROWS == 0
    n_chunks = n_full + (1 if tail else 0)
    log_g = _ret_log_decay()
    idx = jnp.arange(chunk, dtype=F32)
    dist = idx[:, None] - idx[None, :]
    decay = jnp.where(dist >= 0, jnp.exp(log_g[:, None, None] * jnp.maximum(dist, 0.0)), 0.0)
    q_dec = jnp.broadcast_to(jnp.exp(log_g[:, None] * (idx[None, :] + 1.0))[:, :, None], (RET_HEADS, chunk, dv))
    k_dec = jnp.broadcast_to(jnp.exp(log_g[:, None] * (chunk - 1.0 - idx[None, :]))[:, :, None],
                             (RET_HEADS, chunk, dk))
    seq = lambda w: pl.BlockSpec((1, chunk, w), lambda i, c: (i, c, 0))
    return pl.pallas_call(
        functools.partial(_retention_kernel, chunk=chunk, tail=tail),
        grid=(b, n_chunks),
        in_specs=[seq(RET_HEADS * dk), seq(RET_HEADS * dk), seq(RET_HEADS * dv), seq(RET_HEADS * dv),
                  _const_spec(decay.shape), _const_spec(q_dec.shape), _const_spec(k_dec.shape)],
        out_specs=[seq(RET_HEADS * dv), pl.BlockSpec((1, RET_HEADS, dk, dv), lambda i, c: (i, 0, 0, 0))],
        out_shape=[jax.ShapeDtypeStruct((b, t, RET_HEADS * dv), BF16),
                   jax.ShapeDtypeStruct((b, RET_HEADS, dk, dv), F32)],
        scratch_shapes=[pltpu.VMEM((RET_HEADS, dk, dv), F32),
                        pltpu.VMEM((chunk, RET_HEADS * dk), BF16), pltpu.VMEM((chunk, RET_HEADS * dk), BF16),
                        pltpu.VMEM((chunk, RET_HEADS * dv), BF16), pltpu.VMEM((chunk, RET_HEADS * dv), BF16)],
        compiler_params=_params("arbitrary", "arbitrary"),
        name="retention",
    )(q, k, v, gate, decay, q_dec, k_dec)


def _retention_sample_kernel(q_ref, k_ref, v_ref, gate_ref, dec_ref, qd_ref, kd_ref, s0_ref, og_ref, st_ref,
                             *, tp, n_real):
    b = pl.program_id(0)
    dk = q_ref.shape[1] // RET_HEADS
    dv = v_ref.shape[1] // RET_HEADS
    win = SAMPLE_WINDOW
    per_win = win // tp
    w0 = pl.multiple_of(_div_pow2(b, per_win) * win, win)
    slot = _mod_pow2(b, per_win)
    r0 = pl.multiple_of(slot * tp, tp)
    q0 = pl.multiple_of(w0 + r0, tp)
    row = lax.broadcasted_iota(jnp.int32, (win, 1), 0)
    in_seq = _div_pow2(row, tp) == slot
    g_len = _ret_decay_pow(n_real)
    for h in range(RET_HEADS):
        qh = q_ref[pl.ds(q0, tp), h * dk:(h + 1) * dk]
        k_win = k_ref[pl.ds(w0, win), h * dk:(h + 1) * dk]
        v_win = v_ref[pl.ds(w0, win), h * dv:(h + 1) * dv]
        s = _dot_nt(qh, k_win) * dec_ref[h, pl.ds(r0, tp), :]
        o = _dot(s.astype(BF16), v_win)
        state = s0_ref[0, h]
        o = o + _dot(qh, state.astype(BF16)) * qd_ref[h]
        k_scaled = jnp.where(in_seq, k_win.astype(F32) * kd_ref[h], 0.0).astype(BF16)
        st_ref[0, h] = g_len[h] * state + _dot_tn(k_scaled, v_win)
        o = o * lax.rsqrt(jnp.mean(o * o, axis=-1, keepdims=True) + EPS)
        gate = gate_ref[pl.ds(q0, tp), h * dv:(h + 1) * dv].astype(F32)
        og_ref[:, h * dv:(h + 1) * dv] = (gate * o).astype(og_ref.dtype)


def _retention_sample(q, k, v, gate, s0, *, n_seq, tp, n_real):
    n = q.shape[0]
    dk = q.shape[1] // RET_HEADS
    dv = v.shape[1] // RET_HEADS
    win = SAMPLE_WINDOW
    assert n == n_seq * tp and n % win == 0 and win % tp == 0
    log_g = _ret_log_decay()
    r = jnp.arange(win)
    tok = (r % tp).astype(F32)
    same = (r[:, None] // tp) == (r[None, :] // tp)
    dist = tok[:, None] - tok[None, :]
    decay = jnp.where(same & (dist >= 0), jnp.exp(log_g[:, None, None] * jnp.maximum(dist, 0.0)), 0.0)
    q_dec = jnp.broadcast_to(jnp.exp(log_g[:, None] * (tok[None, :tp] + 1.0))[:, :, None], (RET_HEADS, tp, dv))
    k_dec = jnp.broadcast_to(jnp.exp(log_g[:, None] * (n_real - 1.0 - tok[None, :]))[:, :, None],
                             (RET_HEADS, win, dk))
    state_spec = pl.BlockSpec((1, RET_HEADS, dk, dv), lambda i: (i, 0, 0, 0))
    return pl.pallas_call(
        functools.partial(_retention_sample_kernel, tp=tp, n_real=n_real),
        grid=(n_seq,),
        in_specs=[_const_spec(q.shape), _const_spec(k.shape), _const_spec(v.shape), _const_spec(gate.shape),
                  _const_spec(decay.shape), _const_spec(q_dec.shape), _const_spec(k_dec.shape), state_spec],
        out_specs=[pl.BlockSpec((tp, RET_HEADS * dv), lambda i: (i, 0)), state_spec],
        out_shape=[jax.ShapeDtypeStruct((n, RET_HEADS * dv), BF16),
                   jax.ShapeDtypeStruct((n_seq, RET_HEADS, dk, dv), F32)],
        compiler_params=_params("arbitrary"),
        name="retention_sample",
    )(q, k, v, gate, decay, q_dec, k_dec, s0)


def _mix_ffn_kernel(*refs, seq_len, has_prev, final_norm, full_a):
    refs = list(refs)
    x_ref, og_ref, wo_ref, g2_ref, wa_ref, wg_ref, cw_ref, cb_ref, wout_ref = refs[:9]
    refs = refs[9:]
    p1_ref = p2_ref = fg_ref = None
    if has_prev:
        p1_ref, p2_ref = refs[:2]
        refs = refs[2:]
    if final_norm:
        fg_ref = refs[0]
        refs = refs[1:]
    out_ref, a_ref, carry_ref, acc_ref = refs
    i = pl.program_id(0)
    rows = x_ref.shape[0]
    d_ff = wa_ref.shape[1]
    keep = V7X_SUBLANES

    @pl.when(i == 0)
    def _():
        carry_ref[...] = jnp.zeros(carry_ref.shape, F32)

    h = x_ref[...] + _dot(og_ref[...], wo_ref[...])
    n = _rmsnorm(h, g2_ref[...]).astype(BF16)
    row = lax.broadcasted_iota(jnp.int32, (rows, 1), 0)
    if seq_len % rows == 0:
        t = (i % (seq_len // rows)) * rows + row
    else:
        t = _mod_pow2(row, seq_len)
    acc_ref[...] = h
    for c0 in range(0, d_ff, FFN_COL_CHUNK):
        cs = slice(c0, c0 + FFN_COL_CHUNK)
        a = _dot(n, wa_ref[:, cs])
        gt = _dot(n, wg_ref[:, cs])
        prev = carry_ref[:, cs]
        r1 = jnp.where(row == 0, prev[keep - 1:keep], pltpu.roll(a, 1, axis=0))
        r2 = jnp.where(row == 0, prev[keep - 2:keep - 1],
                       jnp.where(row == 1, prev[keep - 1:keep], pltpu.roll(a, 2, axis=0)))
        am1 = jnp.where(t >= 1, r1, p1_ref[:, cs] if has_prev else 0.0)
        am2 = jnp.where(t >= 2, r2, p2_ref[:, cs] if has_prev else 0.0)
        ac = cw_ref[0:1, cs] * am2 + cw_ref[1:2, cs] * am1 + cw_ref[2:3, cs] * a + cb_ref[:, cs]
        u = (_silu(ac) * gt).astype(BF16)
        acc_ref[...] += _dot(u, wout_ref[cs, :])
        carry_ref[:, cs] = a[rows - keep:, :]
        if full_a:
            a_ref[:, cs] = a
        else:
            a_ref[0, :, cs] = a[rows - keep:, :]
    out = acc_ref[...]
    if final_norm:
        out = _rmsnorm(out, fg_ref[...])
    out_ref[...] = out


def _mix_ffn(x2d, og, wo, g2, wa, wg, conv_w, conv_b, wout, *, row_tile, seq_len, prev=None, final_g=None,
             full_a=False):
    n, d = x2d.shape
    d_ff = wa.shape[1]
    assert d_ff % FFN_COL_CHUNK == 0
    assert seq_len % row_tile == 0 or (row_tile % seq_len == 0 and n == row_tile)
    n_tiles = n // row_tile
    row = lambda w: pl.BlockSpec((row_tile, w), lambda i: (i, 0))
    in_specs = [row(d), row(og.shape[1]), _const_spec(wo.shape), _const_spec((1, d)), _const_spec(wa.shape),
                _const_spec(wg.shape), _const_spec(conv_w.shape), _const_spec((1, d_ff)), _const_spec(wout.shape)]
    args = [x2d, og, wo, g2, wa, wg, conv_w, conv_b, wout]
    if prev is not None:
        in_specs += [row(d_ff), row(d_ff)]
        args += list(prev)
    if final_g is not None:
        in_specs.append(_const_spec((1, d)))
        args.append(final_g)
    if full_a:
        a_spec, a_shape = row(d_ff), (n, d_ff)
    else:
        a_spec = pl.BlockSpec((1, V7X_SUBLANES, d_ff), lambda i: (i, 0, 0))
        a_shape = (n_tiles, V7X_SUBLANES, d_ff)
    return pl.pallas_call(
        functools.partial(_mix_ffn_kernel, seq_len=seq_len, has_prev=prev is not None,
                          final_norm=final_g is not None, full_a=full_a),
        grid=(n_tiles,),
        in_specs=in_specs,
        out_specs=[row(d), a_spec],
        out_shape=[jax.ShapeDtypeStruct((n, d), F32), jax.ShapeDtypeStruct(a_shape, F32)],
        scratch_shapes=[pltpu.VMEM((V7X_SUBLANES, d_ff), F32), pltpu.VMEM((row_tile, d), F32)],
        compiler_params=_params("arbitrary"),
        name="mix_ffn",
    )(*args)


def _dif_proj_kernel(x_ref, g_ref, cos_ref, sin_ref, wq_ref, wk_ref, wv_ref,
                     q_ref, k32_ref, v32_ref, k16_ref, v16_ref, *, hd):
    n = _rmsnorm(x_ref[...], g_ref[...]).astype(BF16)
    lane = lax.broadcasted_iota(jnp.int32, (1, V7X_LANES), 1)
    first_half = _mod_pow2(lane, hd) < (hd // 2)
    cos, sin = cos_ref[...], sin_ref[...]

    def rope(y, scale, outs):
        for c0 in range(0, y.shape[1], V7X_LANES):
            blk = y[:, c0:c0 + V7X_LANES]
            other = jnp.where(first_half, pltpu.roll(blk, V7X_LANES - hd // 2, axis=1),
                              pltpu.roll(blk, hd // 2, axis=1))
            out = blk * cos + other * sin
            if scale != 1.0:
                out = out * scale
            for o in outs:
                o[:, c0:c0 + V7X_LANES] = out.astype(o.dtype)

    rope(_dot(n, wq_ref[...]), hd ** -0.5, (q_ref,))
    rope(_dot(n, wk_ref[...]), 1.0, (k32_ref, k16_ref))
    v = _dot(n, wv_ref[...])
    v32_ref[...] = v
    v16_ref[...] = v.astype(BF16)


def _dif_tables(pos, hd):
    half = hd // 2
    inv = ROPE_THETA ** (-jnp.arange(half, dtype=F32) / half)
    th = pos[:, None] * inv[None, :]
    cos, sin = jnp.cos(th), jnp.sin(th)
    reps = V7X_LANES // hd
    return (jnp.tile(jnp.concatenate([cos, cos], axis=1), (1, reps)),
            jnp.tile(jnp.concatenate([-sin, sin], axis=1), (1, reps)))


def _dif_proj(x2d, g, tables, wq, wk, wv, *, row_tile, rows_per_table):
    n, d = x2d.shape
    hd = wq.shape[1] // (2 * DIF_HEADS)
    n_tab = rows_per_table // row_tile
    w = wq.shape[1]
    row = lambda width: pl.BlockSpec((row_tile, width), lambda i: (i, 0))
    tab = pl.BlockSpec((row_tile, V7X_LANES), lambda i: (i % n_tab, 0))
    return pl.pallas_call(
        functools.partial(_dif_proj_kernel, hd=hd),
        grid=(n // row_tile,),
        in_specs=[row(d), _const_spec((1, d)), tab, tab,
                  _const_spec(wq.shape), _const_spec(wk.shape), _const_spec(wv.shape)],
        out_specs=[row(w)] * 5,
        out_shape=[jax.ShapeDtypeStruct((n, w), BF16), jax.ShapeDtypeStruct((n, w), F32),
                   jax.ShapeDtypeStruct((n, w), F32), jax.ShapeDtypeStruct((n, w), BF16),
                   jax.ShapeDtypeStruct((n, w), BF16)],
        compiler_params=_params("arbitrary"),
        name="dif_proj",
    )(x2d, g, *tables, wq, wk, wv)


def _dif_lambda(lq1, lk1, lq2, lk2):
    e1 = jnp.exp(jnp.sum(lq1 * lk1, axis=-1, keepdims=True))
    e2 = jnp.exp(jnp.sum(lq2 * lk2, axis=-1, keepdims=True))
    return e1 - e2 + DIF_LAMBDA_INIT


def _subln(o, g):
    return o * lax.rsqrt(jnp.mean(o * o, axis=-1, keepdims=True) + EPS) * g * (1.0 - DIF_LAMBDA_INIT)


def _dif_flash_kernel(q_ref, k_ref, v_ref, lq1_ref, lk1_ref, lq2_ref, lk2_ref, sg_ref, o_ref,
                      kk_ref, vt_ref, qt_ref, pad_ref, *, block, lead, lead_pad, hd):
    t = q_ref.shape[1]
    hw = 2 * hd
    n_blocks = (t - lead) // block
    off = lead_pad if lead else 0
    lam = _dif_lambda(lq1_ref[...], lk1_ref[...], lq2_ref[...], lk2_ref[...])
    lane = lax.broadcasted_iota(jnp.int32, (1, hw), 1)

    def transposed(x):
        return x.astype(F32).T.astype(BF16)

    def padded_lead(ref):
        pad_ref[...] = jnp.zeros(pad_ref.shape, BF16)
        pad_ref[:lead, :] = ref[0, :lead, :]
        return pad_ref[...]

    def put(row0, n, kb, vb, qb):
        z = jnp.zeros_like(kb)
        kk_ref[0, row0:row0 + n, :] = jnp.where(lane < hd, kb, z)
        kk_ref[1, row0:row0 + n, :] = jnp.where(lane >= hd, kb, z)
        vt_ref[:, row0:row0 + n] = transposed(vb)
        qt_ref[:, row0:row0 + n] = transposed(qb)

    if lead:
        put(0, lead_pad, padded_lead(k_ref), padded_lead(v_ref), padded_lead(q_ref))
    for j in range(n_blocks):
        rows = slice(lead + j * block, lead + (j + 1) * block)
        put(off + j * block, block, k_ref[0, rows, :], v_ref[0, rows, :], q_ref[0, rows, :])

    def attend(q0, n_q, n_keys, out_row0, out_rows):
        qt = qt_ref[:, q0:q0 + n_q]
        diag = n_keys - n_q
        key = lax.broadcasted_iota(jnp.int32, (n_q, n_q), 0)
        qry = lax.broadcasted_iota(jnp.int32, (n_q, n_q), 1)
        maps = []
        for c in range(2):
            s = _dot(kk_ref[c, :n_keys, :], qt)
            parts = []
            if lead and diag > 0:
                lead_key = lax.broadcasted_iota(jnp.int32, (off, n_q), 0)
                parts.append(jnp.where(lead_key < lead, s[:off], NEG_INF))
                if diag > off:
                    parts.append(s[off:diag])
            elif diag > 0:
                parts.append(s[:diag])
            parts.append(jnp.where(key <= qry, s[diag:], NEG_INF))
            s = jnp.concatenate(parts, axis=0) if len(parts) > 1 else parts[0]
            p = jnp.exp(s - jnp.max(s, axis=0, keepdims=True))
            denom = jnp.sum(p, axis=0, keepdims=True)
            maps.append(_dot(vt_ref[:, :n_keys], p.astype(BF16)) / denom)
        a = maps[0] - lam * maps[1]
        a = a * lax.rsqrt(jnp.mean(a * a, axis=0, keepdims=True) + EPS) * sg_ref[...] * (1.0 - DIF_LAMBDA_INIT)
        o_ref[0, out_row0:out_row0 + out_rows, :] = a.T[:out_rows].astype(o_ref.dtype)

    if lead:
        attend(0, lead_pad, lead_pad, 0, lead)
    for i in range(n_blocks):
        attend(off + i * block, block, off + (i + 1) * block, lead + i * block, block)


def _dif_flash(q, k, v, lams, subln_g_col, *, block):
    b, t, w = q.shape
    hw = w // DIF_HEADS
    hd = hw // 2
    lead = t % block
    lead_pad = V7X_LANES
    assert lead % V7X_BF16_ROWS == 0 and lead <= lead_pad
    t_pad = t - lead + (lead_pad if lead else 0)
    head = pl.BlockSpec((1, t, hw), lambda i, h: (i, 0, h))
    vec = _const_spec((1, hd))
    return pl.pallas_call(
        functools.partial(_dif_flash_kernel, block=block, lead=lead, lead_pad=lead_pad, hd=hd),
        grid=(b, DIF_HEADS),
        in_specs=[head, head, head, vec, vec, vec, vec, _const_spec((hw, 1))],
        out_specs=head,
        out_shape=jax.ShapeDtypeStruct((b, t, w), BF16),
        scratch_shapes=[pltpu.VMEM((2, t_pad, hw), BF16), pltpu.VMEM((hw, t_pad), BF16),
                        pltpu.VMEM((hw, t_pad), BF16), pltpu.VMEM((lead_pad, hw), BF16)],
        compiler_params=_params("arbitrary", "arbitrary"),
        name="dif_flash",
    )(q, k, v, *lams, subln_g_col)


def _dif_decode_kernel(*refs, pages, hd):
    qrep_ref = refs[1]
    kt_refs = refs[2:2 + pages]
    v_refs = refs[2 + pages:2 + 2 * pages]
    (kn_ref, vn_ref, lq1_ref, lk1_ref, lq2_ref, lk2_ref, sg_ref, o_ref,
     qbd_ref, knp_ref, vnp_ref, m_ref, l_ref, acc_ref) = refs[2 + 2 * pages:]
    j = pl.program_id(1)
    n_rows, w = qrep_ref.shape[1:]
    hw = 2 * hd
    per_head = n_rows // DIF_HEADS
    slots = per_head // 2

    @pl.when(j == 0)
    def _():
        r = lax.broadcasted_iota(jnp.int32, (n_rows, w), 0)
        chan = lax.broadcasted_iota(jnp.int32, (n_rows, w), 1)
        qbd_ref[...] = jnp.where(_div_pow2(chan, hd) == _div_pow2(r, slots), qrep_ref[0], 0.0).astype(BF16)
        m_ref[...] = jnp.full(m_ref.shape, NEG_INF, F32)
        l_ref[...] = jnp.zeros(l_ref.shape, F32)
        acc_ref[...] = jnp.zeros(acc_ref.shape, F32)

    def update(s, v_of_head):
        m_prev = m_ref[...]
        m_new = jnp.maximum(m_prev, jnp.max(s, axis=1, keepdims=True))
        corr = jnp.exp(m_prev - m_new)
        p = jnp.exp(s - m_new)
        l_ref[...] = l_ref[...] * corr + jnp.sum(p, axis=1, keepdims=True)
        m_ref[...] = m_new
        for h in range(DIF_HEADS):
            rows = slice(h * per_head, (h + 1) * per_head)
            acc_ref[rows, :] = acc_ref[rows, :] * corr[rows] + _dot(p[rows].astype(BF16), v_of_head(h))

    qbd = qbd_ref[...]
    s = jnp.concatenate([_dot(qbd, r[0].astype(BF16)) for r in kt_refs], axis=1)
    page = v_refs[0].shape[1] // DIF_HEADS
    update(s, lambda h: jnp.concatenate(
        [r[0, pl.ds(h, page, stride=DIF_HEADS), :].astype(BF16) for r in v_refs], axis=0))

    @pl.when(j == pl.num_programs(1) - 1)
    def _():
        n_new = kn_ref.shape[1]
        knp_ref[...] = jnp.zeros(knp_ref.shape, BF16)
        vnp_ref[...] = jnp.zeros(vnp_ref.shape, BF16)
        knp_ref[:n_new, :] = kn_ref[0].astype(BF16)
        vnp_ref[:n_new, :] = vn_ref[0].astype(BF16)
        keys = knp_ref.shape[0]
        key = lax.broadcasted_iota(jnp.int32, (n_rows, keys), 1)
        tok = _mod_pow2(lax.broadcasted_iota(jnp.int32, (n_rows, keys), 0), slots)
        s_new = jnp.where(key <= tok, _dot_nt(qbd_ref[...], knp_ref[...]), NEG_INF)
        update(s_new, lambda h: vnp_ref[:, h * hw:(h + 1) * hw])
        lam = _dif_lambda(lq1_ref[...], lk1_ref[...], lq2_ref[...], lk2_ref[...])
        o = acc_ref[...] / l_ref[...]
        for h in range(DIF_HEADS):
            r0 = h * per_head
            a = o[r0:r0 + slots] - lam * o[r0 + slots:r0 + per_head]
            o_ref[0, :, h * hw:(h + 1) * hw] = _subln(a, sg_ref[...])


def _dif_decode(qrep, cache_kt, cache_v, page_table, k_new, v_new, lams, subln_g, *, pages):
    b, n_rows, w = qrep.shape
    hw = w // DIF_HEADS
    hd = hw // 2
    slots = n_rows // (2 * DIF_HEADS)
    page = cache_kt.shape[2]
    n_pages = page_table.shape[1]
    assert n_pages % pages == 0 and k_new.shape[1] <= V7X_LANES
    kt_spec = lambda p: pl.BlockSpec((1, w, page), lambda i, j, pt: (pt[i, j * pages + p], 0, 0))
    v_spec = lambda p: pl.BlockSpec((1, page * DIF_HEADS, hw), lambda i, j, pt: (pt[i, j * pages + p], 0, 0))
    per_seq = lambda shape: pl.BlockSpec((1,) + shape, lambda i, j, pt: (i,) + (0,) * len(shape))
    const = lambda shape: pl.BlockSpec(shape, lambda i, j, pt: (0,) * len(shape))
    grid_spec = pltpu.PrefetchScalarGridSpec(
        num_scalar_prefetch=1,
        grid=(b, n_pages // pages),
        in_specs=[per_seq((n_rows, w))] + [kt_spec(p) for p in range(pages)] + [v_spec(p) for p in range(pages)]
                 + [per_seq(k_new.shape[1:]), per_seq(v_new.shape[1:])]
                 + [const((1, hd))] * 4 + [const((1, hw))],
        out_specs=per_seq((slots, w)),
        scratch_shapes=[pltpu.VMEM((n_rows, w), BF16), pltpu.VMEM((V7X_LANES, w), BF16),
                        pltpu.VMEM((V7X_LANES, w), BF16), pltpu.VMEM((n_rows, 1), F32),
                        pltpu.VMEM((n_rows, 1), F32), pltpu.VMEM((n_rows, hw), F32)],
    )
    return pl.pallas_call(
        functools.partial(_dif_decode_kernel, pages=pages, hd=hd),
        grid_spec=grid_spec,
        out_shape=jax.ShapeDtypeStruct((b, slots, w), F32),
        compiler_params=_params("arbitrary", "arbitrary"),
        name="dif_decode",
    )(page_table, qrep, *([cache_kt] * pages), *([cache_v] * pages), k_new, v_new, *lams, subln_g)


def _largest_tile(n, cap):
    return max(r for r in range(V7X_BF16_ROWS, cap + 1, V7X_BF16_ROWS) if n % r == 0)


def kernel(x_prompt, x_sample, state_ret, cache_k, cache_v, page_table, state_ffn_conv, meta_tokens, norm1_g,
           norm2_g, final_g, ret_w_q, ret_w_k, ret_w_v, ret_w_g, ret_w_o, dif_w_q, dif_w_k, dif_w_v, dif_lq1,
           dif_lk1, dif_lq2, dif_lk2, dif_subln_g, dif_w_o, ffn_w_in, ffn_conv_w, ffn_conv_b, ffn_w_out):
    b, seq, d = x_prompt.shape
    db, ds, _ = x_sample.shape
    t = seq + N_META
    tp = SAMPLE_PAD_T
    assert ds <= tp
    d_ff = ffn_w_out.shape[1]
    page = cache_k.shape[1]
    past_len = page_table.shape[1] * page
    w_dif = dif_w_q.shape[1]
    hd = w_dif // (2 * DIF_HEADS)
    dk = ret_w_q.shape[1] // RET_HEADS
    bf = lambda a: a.astype(BF16)
    row = lambda a: a.reshape(1, -1).astype(F32)
    seq3 = lambda a, n, length: a.reshape(n, length, a.shape[-1])

    meta = jnp.broadcast_to(meta_tokens.astype(x_prompt.dtype)[None], (b, N_META, d))
    hp = jnp.concatenate([meta, x_prompt], axis=1).reshape(b * t, d)
    hs = jnp.pad(x_sample, ((0, 0), (0, tp - ds), (0, 0))).reshape(db * tp, d)
    pos_p = jnp.arange(t, dtype=F32)
    pos_s = jnp.tile(past_len + jnp.arange(tp, dtype=F32), db)
    tile_p = _largest_tile(t, 1024)
    tile_s = db * tp
    lams = [row(v) for v in (dif_lq1, dif_lk1, dif_lq2, dif_lk2)]
    subln_g = row(dif_subln_g)

    ret_w = [bf(w) for w in (ret_w_q, ret_w_k, ret_w_v, ret_w_g)]
    g1 = row(norm1_g[0])
    qp, kp, vp, gp = _ret_proj(hp, g1, _ret_tables(pos_p, dk), *ret_w, row_tile=tile_p, rows_per_table=t)
    qs, ks, vs, gs = _ret_proj(hs, g1, _ret_tables(pos_s, dk), *ret_w, row_tile=tile_s, rows_per_table=tile_s)
    og_p, ret_state_prompt = _retention(seq3(qp, b, t), seq3(kp, b, t), seq3(vp, b, t), seq3(gp, b, t),
                                        chunk=RET_CHUNK)
    og_s, ret_state_sample = _retention_sample(qs, ks, vs, gs, state_ret.astype(F32), n_seq=db, tp=tp, n_real=ds)

    def ffn_weights(i):
        w_a, w_g = jnp.split(ffn_w_in[i], 2, axis=-1)
        return (row(norm2_g[i]), bf(w_a), bf(w_g), ffn_conv_w[i].astype(F32), row(ffn_conv_b[i]), bf(ffn_w_out[i]))

    def prev_rows(state):
        z = jnp.zeros((db, tp, d_ff), F32)
        p1 = z.at[:, 0].set(state[:, 1])
        p2 = z.at[:, 0].set(state[:, 0]).at[:, 1].set(state[:, 1])
        return p1.reshape(db * tp, d_ff), p2.reshape(db * tp, d_ff)

    def conv_state_prompt(a_tail):
        per_seq = a_tail.reshape(b, t // tile_p, V7X_SUBLANES, d_ff)
        return per_seq[:, -1, V7X_SUBLANES - (CONV_W - 1):, :]

    def conv_state_sample(a_full):
        return a_full.reshape(db, tp, d_ff)[:, ds - (CONV_W - 1):ds, :]

    fw0 = ffn_weights(0)
    hp, a_p0 = _mix_ffn(hp, og_p.reshape(b * t, -1), bf(ret_w_o), *fw0, row_tile=tile_p, seq_len=t)
    hs, a_s0 = _mix_ffn(hs, og_s, bf(ret_w_o), *fw0, row_tile=tile_s, seq_len=tp,
                        prev=prev_rows(state_ffn_conv[0].astype(F32)), full_a=True)

    dif_w = [bf(w) for w in (dif_w_q, dif_w_k, dif_w_v)]
    g1 = row(norm1_g[1])
    q16, k32, v32, k16, v16 = _dif_proj(hp, g1, _dif_tables(pos_p, hd), *dif_w, row_tile=tile_p, rows_per_table=t)
    og_p = _dif_flash(seq3(q16, b, t), seq3(k16, b, t), seq3(v16, b, t), lams,
                      dif_subln_g.reshape(-1, 1).astype(F32), block=ATT_BLOCK)
    k_rows_prompt = k32.reshape(b, t, DIF_HEADS, 2, hd)
    v_rows_prompt = v32.reshape(b, t, DIF_HEADS, 2 * hd)

    qs16, ks32, vs32, _, _ = _dif_proj(hs, g1, _dif_tables(pos_s, hd), *dif_w, row_tile=tile_s,
                                       rows_per_table=tile_s)
    ks32 = ks32.reshape(db, tp, w_dif)
    vs32 = vs32.reshape(db, tp, w_dif)
    q_slots = jnp.pad(qs16.reshape(db, tp, w_dif)[:, :ds].astype(F32), ((0, 0), (0, DECODE_SLOTS - ds), (0, 0)))
    qrep = jnp.tile(q_slots, (1, 2 * DIF_HEADS, 1))
    cache_kt = jnp.transpose(cache_k, (0, 2, 3, 4, 1)).reshape(-1, w_dif, page)
    og_s = _dif_decode(qrep, cache_kt, cache_v.reshape(-1, page * DIF_HEADS, 2 * hd), page_table, ks32, vs32, lams, subln_g,
                       pages=DECODE_PAGES_PER_STEP)
    og_s = jnp.pad(og_s[:, :ds], ((0, 0), (0, tp - ds), (0, 0))).astype(BF16)
    k_rows_sample = ks32[:, :ds].reshape(db, ds, DIF_HEADS, 2, hd)
    v_rows_sample = vs32[:, :ds].reshape(db, ds, DIF_HEADS, 2 * hd)

    fw1 = ffn_weights(1)
    fg = row(final_g)
    yp, a_p1 = _mix_ffn(hp, og_p.reshape(b * t, -1), bf(dif_w_o), *fw1, row_tile=tile_p, seq_len=t, final_g=fg)
    ys, a_s1 = _mix_ffn(hs, og_s.reshape(db * tp, -1), bf(dif_w_o), *fw1, row_tile=tile_s, seq_len=tp,
                        prev=prev_rows(state_ffn_conv[1].astype(F32)), final_g=fg, full_a=True)

    y_prompt = yp.reshape(b, t, d)[:, N_META:]
    y_sample = ys.reshape(db, tp, d)[:, :ds]
    conv_p = jnp.stack([conv_state_prompt(a_p0), conv_state_prompt(a_p1)])
    conv_s = jnp.stack([conv_state_sample(a_s0), conv_state_sample(a_s1)])
    return (y_prompt, y_sample, ret_state_prompt, ret_state_sample, k_rows_prompt, v_rows_prompt,
            k_rows_sample, v_rows_sample, conv_p, conv_s)
```

```python
import functools
import math

import jax
import jax.numpy as jnp
from jax import lax
from jax.experimental import pallas as pl
from jax.experimental.pallas import tpu as pltpu

F32 = jnp.float32
BF16 = jnp.bfloat16

N_META = 16
EPS = 1e-6
NEG_INF = -1e30
ROPE_THETA = 10000.0
RET_HEADS = 4
DIF_HEADS = 8
DIF_LAYER = 1
DIF_LAMBDA_INIT = 0.8 - 0.6 * math.exp(-0.3 * DIF_LAYER)
CONV_W = 3

V7X_LANES = 128
V7X_SUBLANES = 8
V7X_BF16_ROWS = 2 * V7X_SUBLANES
V7X_MXU_DIM = 256
V7X_VMEM_LIMIT_BYTES = 56 * 1024 * 1024

RET_CHUNK = V7X_MXU_DIM
ATT_BLOCK = V7X_MXU_DIM
FFN_COL_CHUNK = V7X_MXU_DIM
SAMPLE_PAD_T = V7X_BF16_ROWS
SAMPLE_WINDOW = V7X_LANES
DECODE_PAGES_PER_STEP = 16
DECODE_SLOTS = V7X_SUBLANES


def _params(*sem):
    return pltpu.CompilerParams(dimension_semantics=sem, vmem_limit_bytes=V7X_VMEM_LIMIT_BYTES)


def _const_spec(shape):
    nd = len(shape)
    return pl.BlockSpec(shape, lambda *_: (0,) * nd, pipeline_mode=pl.Buffered(1))


def _log2(n):
    assert n > 0 and n & (n - 1) == 0, n
    return n.bit_length() - 1


def _div_pow2(x, n):
    return lax.shift_right_logical(x, _log2(n))


def _mod_pow2(x, n):
    _log2(n)
    return lax.bitwise_and(x, n - 1)


def _rmsnorm(x, g):
    return x * lax.rsqrt(jnp.mean(x * x, axis=-1, keepdims=True) + EPS) * g


def _silu(x):
    return x * (1.0 / (1.0 + jnp.exp(-x)))


def _dot(a, b):
    return jnp.dot(a, b, preferred_element_type=F32)


def _dot_nt(a, b):
    return lax.dot_general(a, b, (((1,), (1,)), ((), ())), preferred_element_type=F32)


def _dot_tn(a, b):
    return lax.dot_general(a, b, (((0,), (0,)), ((), ())), preferred_element_type=F32)


def _ret_log_decay():
    return jnp.log1p(-jnp.exp2(-5.0 - jnp.arange(RET_HEADS, dtype=F32)))


def _ret_decay_pow(n):
    return [math.exp(n * math.log1p(-2.0 ** (-5.0 - h))) for h in range(RET_HEADS)]


def _ret_proj_kernel(x_ref, g_ref, cos_ref, sa_ref, sb_ref, wq_ref, wk_ref, wv_ref, wg_ref,
                     q_ref, k_ref, v_ref, gate_ref, *, dk):
    n = _rmsnorm(x_ref[...], g_ref[...]).astype(BF16)

    def rotate_into(y, out_ref):
        for c0 in range(0, y.shape[1], V7X_LANES):
            t0 = c0 % dk
            blk = y[:, c0:c0 + V7X_LANES]
            nxt = pltpu.roll(blk, V7X_LANES - 1, axis=1)
            prv = pltpu.roll(blk, 1, axis=1)
            out = (blk * cos_ref[:, t0:t0 + V7X_LANES] + nxt * sa_ref[:, t0:t0 + V7X_LANES]
                   + prv * sb_ref[:, t0:t0 + V7X_LANES])
            out_ref[:, c0:c0 + V7X_LANES] = out.astype(out_ref.dtype)

    rotate_into(_dot(n, wq_ref[...]), q_ref)
    rotate_into(_dot(n, wk_ref[...]) * (dk ** -0.5), k_ref)
    v_ref[...] = _dot(n, wv_ref[...]).astype(v_ref.dtype)
    gate_ref[...] = _silu(_dot(n, wg_ref[...])).astype(gate_ref.dtype)


def _ret_tables(pos, dk):
    angle = 1.0 / (ROPE_THETA ** jnp.linspace(0.0, 1.0, dk // 2, dtype=F32))
    angle = jnp.repeat(angle, 2)
    th = pos[:, None] * angle[None, :]
    cos, sin = jnp.cos(th), jnp.sin(th)
    even = (jnp.arange(dk) % 2 == 0)[None, :]
    return cos, jnp.where(even, -sin, 0.0), jnp.where(even, 0.0, sin)


def _ret_proj(x2d, g, tables, wq, wk, wv, wg, *, row_tile, rows_per_table):
    n, d = x2d.shape
    dk = wq.shape[1] // RET_HEADS
    n_tab = rows_per_table // row_tile
    row = lambda w: pl.BlockSpec((row_tile, w), lambda i: (i, 0))
    tab = pl.BlockSpec((row_tile, dk), lambda i: (i % n_tab, 0))
    return pl.pallas_call(
        functools.partial(_ret_proj_kernel, dk=dk),
        grid=(n // row_tile,),
        in_specs=[row(d), _const_spec((1, d)), tab, tab, tab,
                  _const_spec(wq.shape), _const_spec(wk.shape), _const_spec(wv.shape), _const_spec(wg.shape)],
        out_specs=[row(wq.shape[1]), row(wk.shape[1]), row(wv.shape[1]), row(wg.shape[1])],
        out_shape=[jax.ShapeDtypeStruct((n, w.shape[1]), BF16) for w in (wq, wk, wv, wg)],
        compiler_params=_params("arbitrary"),
        name="ret_proj",
    )(x2d, g, *tables, wq, wk, wv, wg)


def _retention_kernel(q_ref, k_ref, v_ref, gate_ref, dec_ref, qd_ref, kd_ref, og_ref, st_ref,
                      s_ref, qp_ref, kp_ref, vp_ref, gp_ref, *, chunk, tail):
    c = pl.program_id(1)
    last = pl.num_programs(1) - 1
    dk = q_ref.shape[2] // RET_HEADS
    dv = v_ref.shape[2] // RET_HEADS

    @pl.when(c == 0)
    def _():
        s_ref[...] = jnp.zeros(s_ref.shape, F32)

    def step(q2, k2, v2, g2, length):
        g_len = _ret_decay_pow(length)
        k_fix = _ret_decay_pow(length - chunk)
        for h in range(RET_HEADS):
            qh = q2[:, h * dk:(h + 1) * dk]
            kh = k2[:, h * dk:(h + 1) * dk]
            vh = v2[:, h * dv:(h + 1) * dv]
            s = _dot_nt(qh, kh) * dec_ref[h]
            o = _dot(s.astype(BF16), vh)
            state = s_ref[h]
            o = o + _dot(qh, state.astype(BF16)) * qd_ref[h]
            k_scaled = (kh.astype(F32) * kd_ref[h]).astype(BF16)
            upd = _dot_tn(k_scaled, vh)
            if length != chunk:
                upd = upd * k_fix[h]
            s_ref[h] = g_len[h] * state + upd
            o = o * lax.rsqrt(jnp.mean(o * o, axis=-1, keepdims=True) + EPS)
            gate = g2[:, h * dv:(h + 1) * dv].astype(F32)
            og_ref[0, :length, h * dv:(h + 1) * dv] = (gate * o)[:length].astype(og_ref.dtype)

    def full_step():
        step(q_ref[0], k_ref[0], v_ref[0], gate_ref[0], chunk)

    if tail:
        pl.when(c < last)(full_step)

        @pl.when(c == last)
        def _():
            for src, dst in ((q_ref, qp_ref), (k_ref, kp_ref), (v_ref, vp_ref), (gate_ref, gp_ref)):
                dst[...] = jnp.zeros(dst.shape, dst.dtype)
                dst[:tail, :] = src[0, :tail, :]
            step(qp_ref[...], kp_ref[...], vp_ref[...], gp_ref[...], tail)
    else:
        full_step()

    @pl.when(c == last)
    def _():
        st_ref[0] = s_ref[...]


def _retention(q, k, v, gate, *, chunk):
    b, t, _ = q.shape
    dk = q.shape[2] // RET_HEADS
    dv = v.shape[2] // RET_HEADS
    n_full, tail = divmod(t, chunk)
    assert tail % V7X_BF16_ROWS == 0
    n_chunks = n_full + (1 if tail else 0)
    log_g = _ret_log_decay()
    idx = jnp.arange(chunk, dtype=F32)
    dist = idx[:, None] - idx[None, :]
    decay = jnp.where(dist >= 0, jnp.exp(log_g[:, None, None] * jnp.maximum(dist, 0.0)), 0.0)
    q_dec = jnp.broadcast_to(jnp.exp(log_g[:, None] * (idx[None, :] + 1.0))[:, :, None], (RET_HEADS, chunk, dv))
    k_dec = jnp.broadcast_to(jnp.exp(log_g[:, None] * (chunk - 1.0 - idx[None, :]))[:, :, None],
                             (RET_HEADS, chunk, dk))
    seq = lambda w: pl.BlockSpec((1, chunk, w), lambda i, c: (i, c, 0))
    return pl.pallas_call(
        functools.partial(_retention_kernel, chunk=chunk, tail=tail),
        grid=(b, n_chunks),
        in_specs=[seq(RET_HEADS * dk), seq(RET_HEADS * dk), seq(RET_HEADS * dv), seq(RET_HEADS * dv),
                  _const_spec(decay.shape), _const_spec(q_dec.shape), _const_spec(k_dec.shape)],
        out_specs=[seq(RET_HEADS * dv), pl.BlockSpec((1, RET_HEADS, dk, dv), lambda i, c: (i, 0, 0, 0))],
        out_shape=[jax.ShapeDtypeStruct((b, t, RET_HEADS * dv), BF16),
                   jax.ShapeDtypeStruct((b, RET_HEADS, dk, dv), F32)],
        scratch_shapes=[pltpu.VMEM((RET_HEADS, dk, dv), F32),
                        pltpu.VMEM((chunk, RET_HEADS * dk), BF16), pltpu.VMEM((chunk, RET_HEADS * dk), BF16),
                        pltpu.VMEM((chunk, RET_HEADS * dv), BF16), pltpu.VMEM((chunk, RET_HEADS * dv), BF16)],
        compiler_params=_params("arbitrary", "arbitrary"),
        name="retention",
    )(q, k, v, gate, decay, q_dec, k_dec)


def _retention_sample_kernel(q_ref, k_ref, v_ref, gate_ref, dec_ref, qd_ref, kd_ref, s0_ref, og_ref, st_ref,
                             *, tp, n_real):
    b = pl.program_id(0)
    dk = q_ref.shape[1] // RET_HEADS
    dv = v_ref.shape[1] // RET_HEADS
    win = SAMPLE_WINDOW
    per_win = win // tp
    w0 = pl.multiple_of(_div_pow2(b, per_win) * win, win)
    slot = _mod_pow2(b, per_win)
    r0 = pl.multiple_of(slot * tp, tp)
    q0 = pl.multiple_of(w0 + r0, tp)
    row = lax.broadcasted_iota(jnp.int32, (win, 1), 0)
    in_seq = _div_pow2(row, tp) == slot
    g_len = _ret_decay_pow(n_real)
    for h in range(RET_HEADS):
        qh = q_ref[pl.ds(q0, tp), h * dk:(h + 1) * dk]
        k_win = k_ref[pl.ds(w0, win), h * dk:(h + 1) * dk]
        v_win = v_ref[pl.ds(w0, win), h * dv:(h + 1) * dv]
        s = _dot_nt(qh, k_win) * dec_ref[h, pl.ds(r0, tp), :]
        o = _dot(s.astype(BF16), v_win)
        state = s0_ref[0, h]
        o = o + _dot(qh, state.astype(BF16)) * qd_ref[h]
        k_scaled = jnp.where(in_seq, k_win.astype(F32) * kd_ref[h], 0.0).astype(BF16)
        st_ref[0, h] = g_len[h] * state + _dot_tn(k_scaled, v_win)
        o = o * lax.rsqrt(jnp.mean(o * o, axis=-1, keepdims=True) + EPS)
        gate = gate_ref[pl.ds(q0, tp), h * dv:(h + 1) * dv].astype(F32)
        og_ref[:, h * dv:(h + 1) * dv] = (gate * o).astype(og_ref.dtype)


def _retention_sample(q, k, v, gate, s0, *, n_seq, tp, n_real):
    n = q.shape[0]
    dk = q.shape[1] // RET_HEADS
    dv = v.shape[1] // RET_HEADS
    win = SAMPLE_WINDOW
    assert n == n_seq * tp and n % win == 0 and win % tp == 0
    log_g = _ret_log_decay()
    r = jnp.arange(win)
    tok = (r % tp).astype(F32)
    same = (r[:, None] // tp) == (r[None, :] // tp)
    dist = tok[:, None] - tok[None, :]
    decay = jnp.where(same & (dist >= 0), jnp.exp(log_g[:, None, None] * jnp.maximum(dist, 0.0)), 0.0)
    q_dec = jnp.broadcast_to(jnp.exp(log_g[:, None] * (tok[None, :tp] + 1.0))[:, :, None], (RET_HEADS, tp, dv))
    k_dec = jnp.broadcast_to(jnp.exp(log_g[:, None] * (n_real - 1.0 - tok[None, :]))[:, :, None],
                             (RET_HEADS, win, dk))
    state_spec = pl.BlockSpec((1, RET_HEADS, dk, dv), lambda i: (i, 0, 0, 0))
    return pl.pallas_call(
        functools.partial(_retention_sample_kernel, tp=tp, n_real=n_real),
        grid=(n_seq,),
        in_specs=[_const_spec(q.shape), _const_spec(k.shape), _const_spec(v.shape), _const_spec(gate.shape),
                  _const_spec(decay.shape), _const_spec(q_dec.shape), _const_spec(k_dec.shape), state_spec],
        out_specs=[pl.BlockSpec((tp, RET_HEADS * dv), lambda i: (i, 0)), state_spec],
        out_shape=[jax.ShapeDtypeStruct((n, RET_HEADS * dv), BF16),
                   jax.ShapeDtypeStruct((n_seq, RET_HEADS, dk, dv), F32)],
        compiler_params=_params("arbitrary"),
        name="retention_sample",
    )(q, k, v, gate, decay, q_dec, k_dec, s0)


def _mix_ffn_kernel(*refs, seq_len, has_prev, final_norm, full_a):
    refs = list(refs)
    x_ref, og_ref, wo_ref, g2_ref, wa_ref, wg_ref, cw_ref, cb_ref, wout_ref = refs[:9]
    refs = refs[9:]
    p1_ref = p2_ref = fg_ref = None
    if has_prev:
        p1_ref, p2_ref = refs[:2]
        refs = refs[2:]
    if final_norm:
        fg_ref = refs[0]
        refs = refs[1:]
    out_ref, a_ref, carry_ref, acc_ref, ag_ref = refs
    i = pl.program_id(0)
    rows = x_ref.shape[0]
    d_ff = wa_ref.shape[1]
    keep = V7X_SUBLANES
    tile_in_sequence = seq_len % rows == 0

    @pl.when(i == 0)
    def _():
        carry_ref[...] = jnp.zeros(carry_ref.shape, F32)

    h = x_ref[...] + _dot(og_ref[...], wo_ref[...])
    n = _rmsnorm(h, g2_ref[...]).astype(BF16)
    row = lax.broadcasted_iota(jnp.int32, (rows, 1), 0)
    if tile_in_sequence:
        t = (i % (seq_len // rows)) * rows + row
    else:
        t = _mod_pow2(row, seq_len)

    def conv(am2, am1, a0, cs):
        return cw_ref[0:1, cs] * am2 + cw_ref[1:2, cs] * am1 + cw_ref[2:3, cs] * a0 + cb_ref[:, cs]

    def shifted(x, prev, row_x):
        x1 = jnp.where(row_x == 0, prev[keep - 1:keep], pltpu.roll(x, 1, axis=0))
        x2 = jnp.where(row_x == 0, prev[keep - 2:keep - 1],
                       jnp.where(row_x == 1, prev[keep - 1:keep], pltpu.roll(x, 2, axis=0)))
        return x1, x2

    acc_ref[...] = h
    for j, c0 in enumerate(range(0, d_ff, FFN_COL_CHUNK)):
        cs = slice(c0, c0 + FFN_COL_CHUNK)
        a_buf, g_buf = ag_ref.at[j % 2, 0], ag_ref.at[j % 2, 1]
        a_buf[...] = _dot(n, wa_ref[:, cs])
        g_buf[...] = _dot(n, wg_ref[:, cs])
        a, gt = a_buf[...], g_buf[...]
        prev = carry_ref[:, cs]
        if tile_in_sequence and not has_prev:
            ac = conv(pltpu.roll(a, 2, axis=0), pltpu.roll(a, 1, axis=0), a, cs)
            head1, head2 = shifted(a[:keep], prev, row[:keep])
            ac_head = conv(jnp.where(t[:keep] >= 2, head2, 0.0), jnp.where(t[:keep] >= 1, head1, 0.0),
                           a[:keep], cs)
            ac = jnp.concatenate([ac_head, ac[keep:]], axis=0)
        else:
            r1, r2 = shifted(a, prev, row)
            am1 = jnp.where(t >= 1, r1, p1_ref[:, cs] if has_prev else 0.0)
            am2 = jnp.where(t >= 2, r2, p2_ref[:, cs] if has_prev else 0.0)
            ac = conv(am2, am1, a, cs)
        u = (_silu(ac) * gt).astype(BF16)
        acc_ref[...] += _dot(u, wout_ref[cs, :])
        carry_ref[:, cs] = a[rows - keep:, :]
        if full_a:
            a_ref[:, cs] = a
        else:
            a_ref[0, :, cs] = a[rows - keep:, :]
    out = acc_ref[...]
    if final_norm:
        out = _rmsnorm(out, fg_ref[...])
    out_ref[...] = out


def _mix_ffn(x2d, og, wo, g2, wa, wg, conv_w, conv_b, wout, *, row_tile, seq_len, prev=None, final_g=None,
             full_a=False):
    n, d = x2d.shape
    d_ff = wa.shape[1]
    assert d_ff % FFN_COL_CHUNK == 0
    assert seq_len % row_tile == 0 or (row_tile % seq_len == 0 and n == row_tile)
    n_tiles = n // row_tile
    row = lambda w: pl.BlockSpec((row_tile, w), lambda i: (i, 0))
    in_specs = [row(d), row(og.shape[1]), _const_spec(wo.shape), _const_spec((1, d)), _const_spec(wa.shape),
                _const_spec(wg.shape), _const_spec(conv_w.shape), _const_spec((1, d_ff)), _const_spec(wout.shape)]
    args = [x2d, og, wo, g2, wa, wg, conv_w, conv_b, wout]
    if prev is not None:
        in_specs += [row(d_ff), row(d_ff)]
        args += list(prev)
    if final_g is not None:
        in_specs.append(_const_spec((1, d)))
        args.append(final_g)
    if full_a:
        a_spec, a_shape = row(d_ff), (n, d_ff)
    else:
        a_spec = pl.BlockSpec((1, V7X_SUBLANES, d_ff), lambda i: (i, 0, 0))
        a_shape = (n_tiles, V7X_SUBLANES, d_ff)
    return pl.pallas_call(
        functools.partial(_mix_ffn_kernel, seq_len=seq_len, has_prev=prev is not None,
                          final_norm=final_g is not None, full_a=full_a),
        grid=(n_tiles,),
        in_specs=in_specs,
        out_specs=[row(d), a_spec],
        out_shape=[jax.ShapeDtypeStruct((n, d), F32), jax.ShapeDtypeStruct(a_shape, F32)],
        scratch_shapes=[pltpu.VMEM((V7X_SUBLANES, d_ff), F32), pltpu.VMEM((row_tile, d), F32),
                        pltpu.VMEM((2, 2, row_tile, FFN_COL_CHUNK), F32)],
        compiler_params=_params("arbitrary"),
        name="mix_ffn",
    )(*args)


def _dif_proj_kernel(x_ref, g_ref, cos_ref, sin_ref, wq_ref, wk_ref, wv_ref,
                     q_ref, k_ref, v_ref, *, hd):
    n = _rmsnorm(x_ref[...], g_ref[...]).astype(BF16)
    lane = lax.broadcasted_iota(jnp.int32, (1, V7X_LANES), 1)
    first_half = _mod_pow2(lane, hd) < (hd // 2)
    cos, sin = cos_ref[...], sin_ref[...]

    def rope(y, scale, out_ref):
        for c0 in range(0, y.shape[1], V7X_LANES):
            blk = y[:, c0:c0 + V7X_LANES]
            other = jnp.where(first_half, pltpu.roll(blk, V7X_LANES - hd // 2, axis=1),
                              pltpu.roll(blk, hd // 2, axis=1))
            out = blk * cos + other * sin
            if scale != 1.0:
                out = out * scale
            out_ref[:, c0:c0 + V7X_LANES] = out.astype(out_ref.dtype)

    rope(_dot(n, wq_ref[...]), hd ** -0.5, q_ref)
    rope(_dot(n, wk_ref[...]), 1.0, k_ref)
    v_ref[...] = _dot(n, wv_ref[...])


def _dif_tables(pos, hd):
    half = hd // 2
    inv = ROPE_THETA ** (-jnp.arange(half, dtype=F32) / half)
    th = pos[:, None] * inv[None, :]
    cos, sin = jnp.cos(th), jnp.sin(th)
    reps = V7X_LANES // hd
    return (jnp.tile(jnp.concatenate([cos, cos], axis=1), (1, reps)),
            jnp.tile(jnp.concatenate([-sin, sin], axis=1), (1, reps)))


def _dif_proj(x2d, g, tables, wq, wk, wv, *, row_tile, rows_per_table):
    n, d = x2d.shape
    hd = wq.shape[1] // (2 * DIF_HEADS)
    n_tab = rows_per_table // row_tile
    w = wq.shape[1]
    row = lambda width: pl.BlockSpec((row_tile, width), lambda i: (i, 0))
    tab = pl.BlockSpec((row_tile, V7X_LANES), lambda i: (i % n_tab, 0))
    return pl.pallas_call(
        functools.partial(_dif_proj_kernel, hd=hd),
        grid=(n // row_tile,),
        in_specs=[row(d), _const_spec((1, d)), tab, tab,
                  _const_spec(wq.shape), _const_spec(wk.shape), _const_spec(wv.shape)],
        out_specs=[row(w)] * 3,
        out_shape=[jax.ShapeDtypeStruct((n, w), BF16), jax.ShapeDtypeStruct((n, w), F32),
                   jax.ShapeDtypeStruct((n, w), F32)],
        compiler_params=_params("arbitrary"),
        name="dif_proj",
    )(x2d, g, *tables, wq, wk, wv)


def _dif_lambda(lq1, lk1, lq2, lk2):
    e1 = jnp.exp(jnp.sum(lq1 * lk1, axis=-1, keepdims=True))
    e2 = jnp.exp(jnp.sum(lq2 * lk2, axis=-1, keepdims=True))
    return e1 - e2 + DIF_LAMBDA_INIT


def _subln(o, g):
    return o * lax.rsqrt(jnp.mean(o * o, axis=-1, keepdims=True) + EPS) * g * (1.0 - DIF_LAMBDA_INIT)


def _dif_flash_kernel(q_ref, k_ref, v_ref, lq1_ref, lk1_ref, lq2_ref, lk2_ref, sg_ref, o_ref, kt_ref,
                      kk_ref, vt_ref, qt_ref, pad_ref, s_ref, p_ref, *, block, lead, lead_pad, hd):
    t = q_ref.shape[1]
    hw = 2 * hd
    n_blocks = (t - lead) // block
    off = lead_pad if lead else 0
    lam = _dif_lambda(lq1_ref[...], lk1_ref[...], lq2_ref[...], lk2_ref[...])
    lane = lax.broadcasted_iota(jnp.int32, (1, hw), 1)

    def transposed(x):
        return x.astype(F32).T.astype(BF16)

    def padded(ref, row0, n):
        pad_ref[...] = jnp.zeros(pad_ref.shape, F32)
        pad_ref[:n, :] = ref[0, row0:row0 + n, :].astype(F32)
        return pad_ref[...]

    def put(row0, n, kb, vb, qb):
        kb = kb.astype(BF16)
        z = jnp.zeros_like(kb)
        kk_ref[0, row0:row0 + n, :] = jnp.where(lane < hd, kb, z)
        kk_ref[1, row0:row0 + n, :] = jnp.where(lane >= hd, kb, z)
        vt_ref[:, row0:row0 + n] = transposed(vb)
        qt_ref[:, row0:row0 + n] = transposed(qb)

    if lead:
        put(0, lead_pad, padded(k_ref, 0, lead), padded(v_ref, 0, lead), padded(q_ref, 0, lead))
    for j in range(n_blocks):
        rows = slice(lead + j * block, lead + (j + 1) * block)
        put(off + j * block, block, k_ref[0, rows, :], v_ref[0, rows, :], q_ref[0, rows, :])

    for j in range(n_blocks):
        kt_ref[0, :, j * block:(j + 1) * block] = k_ref[0, j * block:(j + 1) * block, :].T
    if lead:
        kt_ref[0, :, t - lead:] = padded(k_ref, t - lead, lead).T[:, :lead]

    def attend(q0, n_q, n_keys, out_row0, out_rows, slot):
        qt = qt_ref[:, q0:q0 + n_q]
        diag = n_keys - n_q
        key = lax.broadcasted_iota(jnp.int32, (n_q, n_q), 0)
        qry = lax.broadcasted_iota(jnp.int32, (n_q, n_q), 1)
        for c in range(2):
            s_ref[slot, c, :n_keys, :n_q] = _dot(kk_ref[c, :n_keys, :], qt)
        maps = []
        for c in range(2):
            if lead and diag > 0:
                lead_key = lax.broadcasted_iota(jnp.int32, (off, n_q), 0)
                s_ref[slot, c, :off, :n_q] = jnp.where(lead_key < lead, s_ref[slot, c, :off, :n_q], NEG_INF)
            s_ref[slot, c, diag:n_keys, :n_q] = jnp.where(key <= qry, s_ref[slot, c, diag:n_keys, :n_q], NEG_INF)
            s = s_ref[slot, c, :n_keys, :n_q]
            p = jnp.exp(s - jnp.max(s, axis=0, keepdims=True))
            denom = jnp.sum(p, axis=0, keepdims=True)
            p_ref[slot, c, :n_keys, :n_q] = p.astype(BF16)
            maps.append(_dot(vt_ref[:, :n_keys], p_ref[slot, c, :n_keys, :n_q]) / denom)
        a = maps[0] - lam * maps[1]
        a = a * lax.rsqrt(jnp.mean(a * a, axis=0, keepdims=True) + EPS) * sg_ref[...] * (1.0 - DIF_LAMBDA_INIT)
        o_ref[0, out_row0:out_row0 + out_rows, :] = a.T[:out_rows].astype(o_ref.dtype)

    if lead:
        attend(0, lead_pad, lead_pad, 0, lead, 0)
    for i in range(n_blocks):
        attend(off + i * block, block, off + (i + 1) * block, lead + i * block, block, (i + 1) % 2)


def _dif_flash(q, k, v, lams, subln_g_col, *, block):
    b, t, w = q.shape
    hw = w // DIF_HEADS
    hd = hw // 2
    lead = t % block
    lead_pad = V7X_LANES
    assert lead % V7X_BF16_ROWS == 0 and lead <= lead_pad
    t_pad = t - lead + (lead_pad if lead else 0)
    head = pl.BlockSpec((1, t, hw), lambda i, h: (i, 0, h))
    vec = _const_spec((1, hd))
    return pl.pallas_call(
        functools.partial(_dif_flash_kernel, block=block, lead=lead, lead_pad=lead_pad, hd=hd),
        grid=(b, DIF_HEADS),
        in_specs=[head, head, head, vec, vec, vec, vec, _const_spec((hw, 1))],
        out_specs=[head, pl.BlockSpec((1, hw, t), lambda i, h: (i, h, 0))],
        out_shape=[jax.ShapeDtypeStruct((b, t, w), BF16), jax.ShapeDtypeStruct((b, w, t), F32)],
        scratch_shapes=[pltpu.VMEM((2, t_pad, hw), BF16), pltpu.VMEM((hw, t_pad), BF16),
                        pltpu.VMEM((hw, t_pad), BF16), pltpu.VMEM((lead_pad, hw), F32),
                        pltpu.VMEM((2, 2, t_pad, block), F32), pltpu.VMEM((2, 2, t_pad, block), BF16)],
        compiler_params=_params("arbitrary", "arbitrary"),
        name="dif_flash",
    )(q, k, v, *lams, subln_g_col)


def _dif_decode_kernel(*refs, pages, hd):
    qrep_ref = refs[1]
    kt_refs = refs[2:2 + pages]
    v_refs = refs[2 + pages:2 + 2 * pages]
    (kn_ref, vn_ref, lq1_ref, lk1_ref, lq2_ref, lk2_ref, sg_ref, o_ref,
     qbd_ref, knp_ref, vnp_ref, m_ref, l_ref, acc_ref) = refs[2 + 2 * pages:]
    j = pl.program_id(1)
    n_rows, w = qrep_ref.shape[1:]
    hw = 2 * hd
    per_head = n_rows // DIF_HEADS
    slots = per_head // 2

    @pl.when(j == 0)
    def _():
        r = lax.broadcasted_iota(jnp.int32, (n_rows, w), 0)
        chan = lax.broadcasted_iota(jnp.int32, (n_rows, w), 1)
        qbd_ref[...] = jnp.where(_div_pow2(chan, hd) == _div_pow2(r, slots), qrep_ref[0], 0.0).astype(BF16)
        m_ref[...] = jnp.full(m_ref.shape, NEG_INF, F32)
        l_ref[...] = jnp.zeros(l_ref.shape, F32)
        acc_ref[...] = jnp.zeros(acc_ref.shape, F32)

    def update(s, v_of_head):
        m_prev = m_ref[...]
        m_new = jnp.maximum(m_prev, jnp.max(s, axis=1, keepdims=True))
        corr = jnp.exp(m_prev - m_new)
        p = jnp.exp(s - m_new)
        l_ref[...] = l_ref[...] * corr + jnp.sum(p, axis=1, keepdims=True)
        m_ref[...] = m_new
        for h in range(DIF_HEADS):
            rows = slice(h * per_head, (h + 1) * per_head)
            acc_ref[rows, :] = acc_ref[rows, :] * corr[rows] + _dot(p[rows].astype(BF16), v_of_head(h))

    qbd = qbd_ref[...]
    s = jnp.concatenate([_dot(qbd, r[0].astype(BF16)) for r in kt_refs], axis=1)
    page = v_refs[0].shape[1] // DIF_HEADS
    update(s, lambda h: jnp.concatenate(
        [r[0, pl.ds(h, page, stride=DIF_HEADS), :].astype(BF16) for r in v_refs], axis=0))

    @pl.when(j == pl.num_programs(1) - 1)
    def _():
        n_new = kn_ref.shape[1]
        knp_ref[...] = jnp.zeros(knp_ref.shape, BF16)
        vnp_ref[...] = jnp.zeros(vnp_ref.shape, BF16)
        knp_ref[:n_new, :] = kn_ref[0].astype(BF16)
        vnp_ref[:n_new, :] = vn_ref[0].astype(BF16)
        keys = knp_ref.shape[0]
        key = lax.broadcasted_iota(jnp.int32, (n_rows, keys), 1)
        tok = _mod_pow2(lax.broadcasted_iota(jnp.int32, (n_rows, keys), 0), slots)
        s_new = jnp.where(key <= tok, _dot_nt(qbd_ref[...], knp_ref[...]), NEG_INF)
        update(s_new, lambda h: vnp_ref[:, h * hw:(h + 1) * hw])
        lam = _dif_lambda(lq1_ref[...], lk1_ref[...], lq2_ref[...], lk2_ref[...])
        o = acc_ref[...] / l_ref[...]
        for h in range(DIF_HEADS):
            r0 = h * per_head
            a = o[r0:r0 + slots] - lam * o[r0 + slots:r0 + per_head]
            o_ref[0, :, h * hw:(h + 1) * hw] = _subln(a, sg_ref[...])


def _dif_decode(qrep, cache_kt, cache_v, page_table, k_new, v_new, lams, subln_g, *, pages):
    b, n_rows, w = qrep.shape
    hw = w // DIF_HEADS
    hd = hw // 2
    slots = n_rows // (2 * DIF_HEADS)
    page = cache_kt.shape[2]
    n_pages = page_table.shape[1]
    assert n_pages % pages == 0 and k_new.shape[1] <= V7X_LANES
    kt_spec = lambda p: pl.BlockSpec((1, w, page), lambda i, j, pt: (pt[i, j * pages + p], 0, 0))
    v_spec = lambda p: pl.BlockSpec((1, page * DIF_HEADS, hw), lambda i, j, pt: (pt[i, j * pages + p], 0, 0))
    per_seq = lambda shape: pl.BlockSpec((1,) + shape, lambda i, j, pt: (i,) + (0,) * len(shape))
    const = lambda shape: pl.BlockSpec(shape, lambda i, j, pt: (0,) * len(shape))
    grid_spec = pltpu.PrefetchScalarGridSpec(
        num_scalar_prefetch=1,
        grid=(b, n_pages // pages),
        in_specs=[per_seq((n_rows, w))] + [kt_spec(p) for p in range(pages)] + [v_spec(p) for p in range(pages)]
                 + [per_seq(k_new.shape[1:]), per_seq(v_new.shape[1:])]
                 + [const((1, hd))] * 4 + [const((1, hw))],
        out_specs=per_seq((slots, w)),
        scratch_shapes=[pltpu.VMEM((n_rows, w), BF16), pltpu.VMEM((V7X_LANES, w), BF16),
                        pltpu.VMEM((V7X_LANES, w), BF16), pltpu.VMEM((n_rows, 1), F32),
                        pltpu.VMEM((n_rows, 1), F32), pltpu.VMEM((n_rows, hw), F32)],
    )
    return pl.pallas_call(
        functools.partial(_dif_decode_kernel, pages=pages, hd=hd),
        grid_spec=grid_spec,
        out_shape=jax.ShapeDtypeStruct((b, slots, w), F32),
        compiler_params=_params("arbitrary", "arbitrary"),
        name="dif_decode",
    )(page_table, qrep, *([cache_kt] * pages), *([cache_v] * pages), k_new, v_new, *lams, subln_g)


def _largest_tile(n, cap):
    return max(r for r in range(V7X_BF16_ROWS, cap + 1, V7X_BF16_ROWS) if n % r == 0)


def kernel(x_prompt, x_sample, state_ret, cache_k, cache_v, page_table, state_ffn_conv, meta_tokens, norm1_g,
           norm2_g, final_g, ret_w_q, ret_w_k, ret_w_v, ret_w_g, ret_w_o, dif_w_q, dif_w_k, dif_w_v, dif_lq1,
           dif_lk1, dif_lq2, dif_lk2, dif_subln_g, dif_w_o, ffn_w_in, ffn_conv_w, ffn_conv_b, ffn_w_out):
    b, seq, d = x_prompt.shape
    db, ds, _ = x_sample.shape
    t = seq + N_META
    tp = SAMPLE_PAD_T
    assert ds <= tp
    d_ff = ffn_w_out.shape[1]
    page = cache_k.shape[1]
    past_len = page_table.shape[1] * page
    w_dif = dif_w_q.shape[1]
    hd = w_dif // (2 * DIF_HEADS)
    dk = ret_w_q.shape[1] // RET_HEADS
    bf = lambda a: a.astype(BF16)
    row = lambda a: a.reshape(1, -1).astype(F32)
    seq3 = lambda a, n, length: a.reshape(n, length, a.shape[-1])

    meta = jnp.broadcast_to(meta_tokens.astype(x_prompt.dtype)[None], (b, N_META, d))
    hp = jnp.concatenate([meta, x_prompt], axis=1).reshape(b * t, d)
    hs = jnp.pad(x_sample, ((0, 0), (0, tp - ds), (0, 0))).reshape(db * tp, d)
    pos_p = jnp.arange(t, dtype=F32)
    pos_s = jnp.tile(past_len + jnp.arange(tp, dtype=F32), db)
    tile_p = _largest_tile(t, 1024)
    tile_s = db * tp
    lams = [row(v) for v in (dif_lq1, dif_lk1, dif_lq2, dif_lk2)]
    subln_g = row(dif_subln_g)

    ret_w = [bf(w) for w in (ret_w_q, ret_w_k, ret_w_v, ret_w_g)]
    g1 = row(norm1_g[0])
    qp, kp, vp, gp = _ret_proj(hp, g1, _ret_tables(pos_p, dk), *ret_w, row_tile=tile_p, rows_per_table=t)
    qs, ks, vs, gs = _ret_proj(hs, g1, _ret_tables(pos_s, dk), *ret_w, row_tile=tile_s, rows_per_table=tile_s)
    og_p, ret_state_prompt = _retention(seq3(qp, b, t), seq3(kp, b, t), seq3(vp, b, t), seq3(gp, b, t),
                                        chunk=RET_CHUNK)
    og_s, ret_state_sample = _retention_sample(qs, ks, vs, gs, state_ret.astype(F32), n_seq=db, tp=tp, n_real=ds)

    def ffn_weights(i):
        w_a, w_g = jnp.split(ffn_w_in[i], 2, axis=-1)
        return (row(norm2_g[i]), bf(w_a), bf(w_g), ffn_conv_w[i].astype(F32), row(ffn_conv_b[i]), bf(ffn_w_out[i]))

    def prev_rows(state):
        z = jnp.zeros((db, tp, d_ff), F32)
        p1 = z.at[:, 0].set(state[:, 1])
        p2 = z.at[:, 0].set(state[:, 0]).at[:, 1].set(state[:, 1])
        return p1.reshape(db * tp, d_ff), p2.reshape(db * tp, d_ff)

    def conv_state_prompt(a_tail):
        per_seq = a_tail.reshape(b, t // tile_p, V7X_SUBLANES, d_ff)
        return per_seq[:, -1, V7X_SUBLANES - (CONV_W - 1):, :]

    def conv_state_sample(a_full):
        return a_full.reshape(db, tp, d_ff)[:, ds - (CONV_W - 1):ds, :]

    fw0 = ffn_weights(0)
    hp, a_p0 = _mix_ffn(hp, og_p.reshape(b * t, -1), bf(ret_w_o), *fw0, row_tile=tile_p, seq_len=t)
    hs, a_s0 = _mix_ffn(hs, og_s, bf(ret_w_o), *fw0, row_tile=tile_s, seq_len=tp,
                        prev=prev_rows(state_ffn_conv[0].astype(F32)), full_a=True)

    dif_w = [bf(w) for w in (dif_w_q, dif_w_k, dif_w_v)]
    g1 = row(norm1_g[1])
    q16, k32, v32 = _dif_proj(hp, g1, _dif_tables(pos_p, hd), *dif_w, row_tile=tile_p, rows_per_table=t)
    og_p, kt_p = _dif_flash(seq3(q16, b, t), seq3(k32, b, t), seq3(v32, b, t), lams,
                            dif_subln_g.reshape(-1, 1).astype(F32), block=ATT_BLOCK)
    k_rows_prompt = jnp.transpose(kt_p.reshape(b, DIF_HEADS, 2, hd, t), (0, 4, 1, 2, 3))
    v_rows_prompt = v32.reshape(b, t, DIF_HEADS, 2 * hd)

    qs16, ks32, vs32 = _dif_proj(hs, g1, _dif_tables(pos_s, hd), *dif_w, row_tile=tile_s, rows_per_table=tile_s)
    ks32 = ks32.reshape(db, tp, w_dif)
    vs32 = vs32.reshape(db, tp, w_dif)
    q_slots = jnp.pad(qs16.reshape(db, tp, w_dif)[:, :ds].astype(F32), ((0, 0), (0, DECODE_SLOTS - ds), (0, 0)))
    qrep = jnp.tile(q_slots, (1, 2 * DIF_HEADS, 1))
    cache_kt = jnp.transpose(cache_k, (0, 2, 3, 4, 1)).reshape(-1, w_dif, page)
    og_s = _dif_decode(qrep, cache_kt, cache_v.reshape(-1, page * DIF_HEADS, 2 * hd), page_table, ks32, vs32, lams, subln_g,
                       pages=DECODE_PAGES_PER_STEP)
    og_s = jnp.pad(og_s[:, :ds], ((0, 0), (0, tp - ds), (0, 0))).astype(BF16)
    k_rows_sample = ks32[:, :ds].reshape(db, ds, DIF_HEADS, 2, hd)
    v_rows_sample = vs32[:, :ds].reshape(db, ds, DIF_HEADS, 2 * hd)

    fw1 = ffn_weights(1)
    fg = row(final_g)
    yp, a_p1 = _mix_ffn(hp, og_p.reshape(b * t, -1), bf(dif_w_o), *fw1, row_tile=tile_p, seq_len=t, final_g=fg)
    ys, a_s1 = _mix_ffn(hs, og_s.reshape(db * tp, -1), bf(dif_w_o), *fw1, row_tile=tile_s, seq_len=tp,
                        prev=prev_rows(state_ffn_conv[1].astype(F32)), final_g=fg, full_a=True)

    y_prompt = yp.reshape(b, t, d)[:, N_META:]
    y_sample = ys.reshape(db, tp, d)[:, :ds]
    conv_p = jnp.stack([conv_state_prompt(a_p0), conv_state_prompt(a_p1)])
    conv_s = jnp.stack([conv_state_sample(a_s0), conv_state_sample(a_s1)])
    return (y_prompt, y_sample, ret_state_prompt, ret_state_sample, k_rows_prompt, v_rows_prompt,
            k_rows_sample, v_rows_sample, conv_p, conv_s)
```

```python
import functools
import math

import jax
import jax.numpy as jnp
from jax import lax
from jax.experimental import pallas as pl
from jax.experimental.pallas import tpu as pltpu

F32 = jnp.float32
BF16 = jnp.bfloat16

N_META = 16
EPS = 1e-6
NEG_INF = -1e30
ROPE_THETA = 10000.0
RET_HEADS = 4
DIF_HEADS = 8
DIF_LAYER = 1
DIF_LAMBDA_INIT = 0.8 - 0.6 * math.exp(-0.3 * DIF_LAYER)
CONV_W = 3

V7X_LANES = 128
V7X_SUBLANES = 8
V7X_BF16_ROWS = 2 * V7X_SUBLANES
V7X_MXU_DIM = 256
V7X_VMEM_LIMIT_BYTES = 56 * 1024 * 1024

RET_CHUNK = V7X_MXU_DIM
ATT_BLOCK = V7X_MXU_DIM
FFN_COL_CHUNK = V7X_MXU_DIM
SAMPLE_PAD_T = V7X_BF16_ROWS
SAMPLE_WINDOW = V7X_LANES
DECODE_PAGES_PER_STEP = 16
DECODE_SLOTS = V7X_SUBLANES // 2
V7X_ATTN_VMEM_LIMIT_BYTES = 60 * 1024 * 1024


def _params(*sem):
    return pltpu.CompilerParams(dimension_semantics=sem, vmem_limit_bytes=V7X_VMEM_LIMIT_BYTES)


def _const_spec(shape):
    nd = len(shape)
    return pl.BlockSpec(shape, lambda *_: (0,) * nd, pipeline_mode=pl.Buffered(1))


def _log2(n):
    assert n > 0 and n & (n - 1) == 0, n
    return n.bit_length() - 1


def _div_pow2(x, n):
    return lax.shift_right_logical(x, _log2(n))


def _mod_pow2(x, n):
    _log2(n)
    return lax.bitwise_and(x, n - 1)


def _rmsnorm(x, g):
    return x * lax.rsqrt(jnp.mean(x * x, axis=-1, keepdims=True) + EPS) * g


def _silu(x):
    return x * (1.0 / (1.0 + jnp.exp(-x)))


def _dot(a, b):
    return jnp.dot(a, b, preferred_element_type=F32)


def _dot_nt(a, b):
    return lax.dot_general(a, b, (((1,), (1,)), ((), ())), preferred_element_type=F32)


def _dot_tn(a, b):
    return lax.dot_general(a, b, (((0,), (0,)), ((), ())), preferred_element_type=F32)


def _ret_log_decay():
    return jnp.log1p(-jnp.exp2(-5.0 - jnp.arange(RET_HEADS, dtype=F32)))


def _ret_decay_pow(n):
    return [math.exp(n * math.log1p(-2.0 ** (-5.0 - h))) for h in range(RET_HEADS)]


def _ret_proj_kernel(x_ref, g_ref, cos_ref, sa_ref, sb_ref, wq_ref, wk_ref, wv_ref, wg_ref,
                     q_ref, k_ref, v_ref, gate_ref, *, dk):
    n = _rmsnorm(x_ref[...], g_ref[...]).astype(BF16)

    def rotate_into(y, out_ref):
        for c0 in range(0, y.shape[1], V7X_LANES):
            t0 = c0 % dk
            blk = y[:, c0:c0 + V7X_LANES]
            nxt = pltpu.roll(blk, V7X_LANES - 1, axis=1)
            prv = pltpu.roll(blk, 1, axis=1)
            out = (blk * cos_ref[:, t0:t0 + V7X_LANES] + nxt * sa_ref[:, t0:t0 + V7X_LANES]
                   + prv * sb_ref[:, t0:t0 + V7X_LANES])
            out_ref[:, c0:c0 + V7X_LANES] = out.astype(out_ref.dtype)

    rotate_into(_dot(n, wq_ref[...]), q_ref)
    rotate_into(_dot(n, wk_ref[...]) * (dk ** -0.5), k_ref)
    v_ref[...] = _dot(n, wv_ref[...]).astype(v_ref.dtype)
    gate_ref[...] = _silu(_dot(n, wg_ref[...])).astype(gate_ref.dtype)


def _ret_tables(pos, dk):
    angle = 1.0 / (ROPE_THETA ** jnp.linspace(0.0, 1.0, dk // 2, dtype=F32))
    angle = jnp.repeat(angle, 2)
    th = pos[:, None] * angle[None, :]
    cos, sin = jnp.cos(th), jnp.sin(th)
    even = (jnp.arange(dk) % 2 == 0)[None, :]
    return cos, jnp.where(even, -sin, 0.0), jnp.where(even, 0.0, sin)


def _ret_proj(x2d, g, tables, wq, wk, wv, wg, *, row_tile, rows_per_table):
    n, d = x2d.shape
    dk = wq.shape[1] // RET_HEADS
    n_tab = rows_per_table // row_tile
    row = lambda w: pl.BlockSpec((row_tile, w), lambda i: (i, 0))
    tab = pl.BlockSpec((row_tile, dk), lambda i: (i % n_tab, 0))
    return pl.pallas_call(
        functools.partial(_ret_proj_kernel, dk=dk),
        grid=(n // row_tile,),
        in_specs=[row(d), _const_spec((1, d)), tab, tab, tab,
                  _const_spec(wq.shape), _const_spec(wk.shape), _const_spec(wv.shape), _const_spec(wg.shape)],
        out_specs=[row(wq.shape[1]), row(wk.shape[1]), row(wv.shape[1]), row(wg.shape[1])],
        out_shape=[jax.ShapeDtypeStruct((n, w.shape[1]), BF16) for w in (wq, wk, wv, wg)],
        compiler_params=_params("arbitrary"),
        name="ret_proj",
    )(x2d, g, *tables, wq, wk, wv, wg)


def _retention_kernel(q_ref, k_ref, v_ref, gate_ref, dec_ref, qd_ref, kd_ref, og_ref, st_ref,
                      s_ref, qp_ref, kp_ref, vp_ref, gp_ref, *, chunk, tail):
    c = pl.program_id(1)
    last = pl.num_programs(1) - 1
    dk = q_ref.shape[2] // RET_HEADS
    dv = v_ref.shape[2] // RET_HEADS

    @pl.when(c == 0)
    def _():
        s_ref[...] = jnp.zeros(s_ref.shape, F32)

    def step(q2, k2, v2, g2, length):
        g_len = _ret_decay_pow(length)
        k_fix = _ret_decay_pow(length - chunk)
        for h in range(RET_HEADS):
            qh = q2[:, h * dk:(h + 1) * dk]
            kh = k2[:, h * dk:(h + 1) * dk]
            vh = v2[:, h * dv:(h + 1) * dv]
            s = _dot_nt(qh, kh) * dec_ref[h]
            o = _dot(s.astype(BF16), vh)
            state = s_ref[h]
            o = o + _dot(qh, state.astype(BF16)) * qd_ref[h]
            k_scaled = (kh.astype(F32) * kd_ref[h]).astype(BF16)
            upd = _dot_tn(k_scaled, vh)
            if length != chunk:
                upd = upd * k_fix[h]
            s_ref[h] = g_len[h] * state + upd
            o = o * lax.rsqrt(jnp.mean(o * o, axis=-1, keepdims=True) + EPS)
            gate = g2[:, h * dv:(h + 1) * dv].astype(F32)
            og_ref[0, :length, h * dv:(h + 1) * dv] = (gate * o)[:length].astype(og_ref.dtype)

    def full_step():
        step(q_ref[0], k_ref[0], v_ref[0], gate_ref[0], chunk)

    if tail:
        pl.when(c < last)(full_step)

        @pl.when(c == last)
        def _():
            for src, dst in ((q_ref, qp_ref), (k_ref, kp_ref), (v_ref, vp_ref), (gate_ref, gp_ref)):
                dst[...] = jnp.zeros(dst.shape, dst.dtype)
                dst[:tail, :] = src[0, :tail, :]
            step(qp_ref[...], kp_ref[...], vp_ref[...], gp_ref[...], tail)
    else:
        full_step()

    @pl.when(c == last)
    def _():
        st_ref[0] = s_ref[...]


def _retention(q, k, v, gate, *, chunk):
    b, t, _ = q.shape
    dk = q.shape[2] // RET_HEADS
    dv = v.shape[2] // RET_HEADS
    n_full, tail = divmod(t, chunk)
    assert tail % V7X_BF16_ROWS == 0
    n_chunks = n_full + (1 if tail else 0)
    log_g = _ret_log_decay()
    idx = jnp.arange(chunk, dtype=F32)
    dist = idx[:, None] - idx[None, :]
    decay = jnp.where(dist >= 0, jnp.exp(log_g[:, None, None] * jnp.maximum(dist, 0.0)), 0.0)
    q_dec = jnp.broadcast_to(jnp.exp(log_g[:, None] * (idx[None, :] + 1.0))[:, :, None], (RET_HEADS, chunk, dv))
    k_dec = jnp.broadcast_to(jnp.exp(log_g[:, None] * (chunk - 1.0 - idx[None, :]))[:, :, None],
                             (RET_HEADS, chunk, dk))
    seq = lambda w: pl.BlockSpec((1, chunk, w), lambda i, c: (i, c, 0))
    return pl.pallas_call(
        functools.partial(_retention_kernel, chunk=chunk, tail=tail),
        grid=(b, n_chunks),
        in_specs=[seq(RET_HEADS * dk), seq(RET_HEADS * dk), seq(RET_HEADS * dv), seq(RET_HEADS * dv),
                  _const_spec(decay.shape), _const_spec(q_dec.shape), _const_spec(k_dec.shape)],
        out_specs=[seq(RET_HEADS * dv), pl.BlockSpec((1, RET_HEADS, dk, dv), lambda i, c: (i, 0, 0, 0))],
        out_shape=[jax.ShapeDtypeStruct((b, t, RET_HEADS * dv), BF16),
                   jax.ShapeDtypeStruct((b, RET_HEADS, dk, dv), F32)],
        scratch_shapes=[pltpu.VMEM((RET_HEADS, dk, dv), F32),
                        pltpu.VMEM((chunk, RET_HEADS * dk), BF16), pltpu.VMEM((chunk, RET_HEADS * dk), BF16),
                        pltpu.VMEM((chunk, RET_HEADS * dv), BF16), pltpu.VMEM((chunk, RET_HEADS * dv), BF16)],
        compiler_params=_params("arbitrary", "arbitrary"),
        name="retention",
    )(q, k, v, gate, decay, q_dec, k_dec)


def _retention_sample_kernel(q_ref, k_ref, v_ref, gate_ref, dec_ref, qd_ref, kd_ref, s0_ref, og_ref, st_ref,
                             *, tp, n_real):
    b = pl.program_id(0)
    dk = q_ref.shape[1] // RET_HEADS
    dv = v_ref.shape[1] // RET_HEADS
    win = SAMPLE_WINDOW
    per_win = win // tp
    w0 = pl.multiple_of(_div_pow2(b, per_win) * win, win)
    slot = _mod_pow2(b, per_win)
    r0 = pl.multiple_of(slot * tp, tp)
    q0 = pl.multiple_of(w0 + r0, tp)
    row = lax.broadcasted_iota(jnp.int32, (win, 1), 0)
    in_seq = _div_pow2(row, tp) == slot
    g_len = _ret_decay_pow(n_real)
    for h in range(RET_HEADS):
        qh = q_ref[pl.ds(q0, tp), h * dk:(h + 1) * dk]
        k_win = k_ref[pl.ds(w0, win), h * dk:(h + 1) * dk]
        v_win = v_ref[pl.ds(w0, win), h * dv:(h + 1) * dv]
        s = _dot_nt(qh, k_win) * dec_ref[h, pl.ds(r0, tp), :]
        o = _dot(s.astype(BF16), v_win)
        state = s0_ref[0, h]
        o = o + _dot(qh, state.astype(BF16)) * qd_ref[h]
        k_scaled = jnp.where(in_seq, k_win.astype(F32) * kd_ref[h], 0.0).astype(BF16)
        st_ref[0, h] = g_len[h] * state + _dot_tn(k_scaled, v_win)
        o = o * lax.rsqrt(jnp.mean(o * o, axis=-1, keepdims=True) + EPS)
        gate = gate_ref[pl.ds(q0, tp), h * dv:(h + 1) * dv].astype(F32)
        og_ref[:, h * dv:(h + 1) * dv] = (gate * o).astype(og_ref.dtype)


def _retention_sample(q, k, v, gate, s0, *, n_seq, tp, n_real):
    n = q.shape[0]
    dk = q.shape[1] // RET_HEADS
    dv = v.shape[1] // RET_HEADS
    win = SAMPLE_WINDOW
    assert n == n_seq * tp and n % win == 0 and win % tp == 0
    log_g = _ret_log_decay()
    r = jnp.arange(win)
    tok = (r % tp).astype(F32)
    same = (r[:, None] // tp) == (r[None, :] // tp)
    dist = tok[:, None] - tok[None, :]
    decay = jnp.where(same & (dist >= 0), jnp.exp(log_g[:, None, None] * jnp.maximum(dist, 0.0)), 0.0)
    q_dec = jnp.broadcast_to(jnp.exp(log_g[:, None] * (tok[None, :tp] + 1.0))[:, :, None], (RET_HEADS, tp, dv))
    k_dec = jnp.broadcast_to(jnp.exp(log_g[:, None] * (n_real - 1.0 - tok[None, :]))[:, :, None],
                             (RET_HEADS, win, dk))
    state_spec = pl.BlockSpec((1, RET_HEADS, dk, dv), lambda i: (i, 0, 0, 0))
    return pl.pallas_call(
        functools.partial(_retention_sample_kernel, tp=tp, n_real=n_real),
        grid=(n_seq,),
        in_specs=[_const_spec(q.shape), _const_spec(k.shape), _const_spec(v.shape), _const_spec(gate.shape),
                  _const_spec(decay.shape), _const_spec(q_dec.shape), _const_spec(k_dec.shape), state_spec],
        out_specs=[pl.BlockSpec((tp, RET_HEADS * dv), lambda i: (i, 0)), state_spec],
        out_shape=[jax.ShapeDtypeStruct((n, RET_HEADS * dv), BF16),
                   jax.ShapeDtypeStruct((n_seq, RET_HEADS, dk, dv), F32)],
        compiler_params=_params("arbitrary"),
        name="retention_sample",
    )(q, k, v, gate, decay, q_dec, k_dec, s0)


def _mix_ffn_kernel(*refs, seq_len, has_prev, final_norm, full_a):
    refs = list(refs)
    x_ref, og_ref, wo_ref, g2_ref, wa_ref, wg_ref, cw_ref, cb_ref, wout_ref = refs[:9]
    refs = refs[9:]
    p1_ref = p2_ref = fg_ref = None
    if has_prev:
        p1_ref, p2_ref = refs[:2]
        refs = refs[2:]
    if final_norm:
        fg_ref = refs[0]
        refs = refs[1:]
    out_ref, a_ref, carry_ref, acc_ref, ag_ref = refs
    i = pl.program_id(0)
    rows = x_ref.shape[0]
    d_ff = wa_ref.shape[1]
    keep = V7X_SUBLANES
    tile_in_sequence = seq_len % rows == 0

    @pl.when(i == 0)
    def _():
        carry_ref[...] = jnp.zeros(carry_ref.shape, F32)

    h = x_ref[...] + _dot(og_ref[...], wo_ref[...])
    n = _rmsnorm(h, g2_ref[...]).astype(BF16)
    row = lax.broadcasted_iota(jnp.int32, (rows, 1), 0)
    if tile_in_sequence:
        t = (i % (seq_len // rows)) * rows + row
    else:
        t = _mod_pow2(row, seq_len)

    def conv(am2, am1, a0, cs):
        return cw_ref[0:1, cs] * am2 + cw_ref[1:2, cs] * am1 + cw_ref[2:3, cs] * a0 + cb_ref[:, cs]

    def shifted(x, prev, row_x):
        x1 = jnp.where(row_x == 0, prev[keep - 1:keep], pltpu.roll(x, 1, axis=0))
        x2 = jnp.where(row_x == 0, prev[keep - 2:keep - 1],
                       jnp.where(row_x == 1, prev[keep - 1:keep], pltpu.roll(x, 2, axis=0)))
        return x1, x2

    acc_ref[...] = h
    for j, c0 in enumerate(range(0, d_ff, FFN_COL_CHUNK)):
        cs = slice(c0, c0 + FFN_COL_CHUNK)
        a_buf, g_buf = ag_ref.at[j % 2, 0], ag_ref.at[j % 2, 1]
        a_buf[...] = _dot(n, wa_ref[:, cs])
        g_buf[...] = _dot(n, wg_ref[:, cs])
        a, gt = a_buf[...], g_buf[...]
        prev = carry_ref[:, cs]
        if tile_in_sequence and not has_prev:
            ac = conv(pltpu.roll(a, 2, axis=0), pltpu.roll(a, 1, axis=0), a, cs)
            head1, head2 = shifted(a[:keep], prev, row[:keep])
            ac_head = conv(jnp.where(t[:keep] >= 2, head2, 0.0), jnp.where(t[:keep] >= 1, head1, 0.0),
                           a[:keep], cs)
            ac = jnp.concatenate([ac_head, ac[keep:]], axis=0)
        else:
            r1, r2 = shifted(a, prev, row)
            am1 = jnp.where(t >= 1, r1, p1_ref[:, cs] if has_prev else 0.0)
            am2 = jnp.where(t >= 2, r2, p2_ref[:, cs] if has_prev else 0.0)
            ac = conv(am2, am1, a, cs)
        u = (_silu(ac) * gt).astype(BF16)
        acc_ref[...] += _dot(u, wout_ref[cs, :])
        carry_ref[:, cs] = a[rows - keep:, :]
        if full_a:
            a_ref[:, cs] = a
        else:
            a_ref[0, :, cs] = a[rows - keep:, :]
    out = acc_ref[...]
    if final_norm:
        out = _rmsnorm(out, fg_ref[...])
    out_ref[...] = out


def _mix_ffn(x2d, og, wo, g2, wa, wg, conv_w, conv_b, wout, *, row_tile, seq_len, prev=None, final_g=None,
             full_a=False):
    n, d = x2d.shape
    d_ff = wa.shape[1]
    assert d_ff % FFN_COL_CHUNK == 0
    assert seq_len % row_tile == 0 or (row_tile % seq_len == 0 and n == row_tile)
    n_tiles = n // row_tile
    row = lambda w: pl.BlockSpec((row_tile, w), lambda i: (i, 0))
    in_specs = [row(d), row(og.shape[1]), _const_spec(wo.shape), _const_spec((1, d)), _const_spec(wa.shape),
                _const_spec(wg.shape), _const_spec(conv_w.shape), _const_spec((1, d_ff)), _const_spec(wout.shape)]
    args = [x2d, og, wo, g2, wa, wg, conv_w, conv_b, wout]
    if prev is not None:
        in_specs += [row(d_ff), row(d_ff)]
        args += list(prev)
    if final_g is not None:
        in_specs.append(_const_spec((1, d)))
        args.append(final_g)
    if full_a:
        a_spec, a_shape = row(d_ff), (n, d_ff)
    else:
        a_spec = pl.BlockSpec((1, V7X_SUBLANES, d_ff), lambda i: (i, 0, 0))
        a_shape = (n_tiles, V7X_SUBLANES, d_ff)
    return pl.pallas_call(
        functools.partial(_mix_ffn_kernel, seq_len=seq_len, has_prev=prev is not None,
                          final_norm=final_g is not None, full_a=full_a),
        grid=(n_tiles,),
        in_specs=in_specs,
        out_specs=[row(d), a_spec],
        out_shape=[jax.ShapeDtypeStruct((n, d), F32), jax.ShapeDtypeStruct(a_shape, F32)],
        scratch_shapes=[pltpu.VMEM((V7X_SUBLANES, d_ff), F32), pltpu.VMEM((row_tile, d), F32),
                        pltpu.VMEM((2, 2, row_tile, FFN_COL_CHUNK), F32)],
        compiler_params=_params("arbitrary"),
        name="mix_ffn",
    )(*args)


def _dif_proj_kernel(x_ref, g_ref, cos_ref, sin_ref, wq_ref, wk_ref, wv_ref,
                     q_ref, k_ref, v_ref, *, hd):
    n = _rmsnorm(x_ref[...], g_ref[...]).astype(BF16)
    lane = lax.broadcasted_iota(jnp.int32, (1, V7X_LANES), 1)
    first_half = _mod_pow2(lane, hd) < (hd // 2)
    cos, sin = cos_ref[...], sin_ref[...]

    def rope(y, scale, out_ref):
        for c0 in range(0, y.shape[1], V7X_LANES):
            blk = y[:, c0:c0 + V7X_LANES]
            other = jnp.where(first_half, pltpu.roll(blk, V7X_LANES - hd // 2, axis=1),
                              pltpu.roll(blk, hd // 2, axis=1))
            out = blk * cos + other * sin
            if scale != 1.0:
                out = out * scale
            out_ref[:, c0:c0 + V7X_LANES] = out.astype(out_ref.dtype)

    rope(_dot(n, wq_ref[...]), hd ** -0.5, q_ref)
    rope(_dot(n, wk_ref[...]), 1.0, k_ref)
    v_ref[...] = _dot(n, wv_ref[...])


def _dif_tables(pos, hd):
    half = hd // 2
    inv = ROPE_THETA ** (-jnp.arange(half, dtype=F32) / half)
    th = pos[:, None] * inv[None, :]
    cos, sin = jnp.cos(th), jnp.sin(th)
    reps = V7X_LANES // hd
    return (jnp.tile(jnp.concatenate([cos, cos], axis=1), (1, reps)),
            jnp.tile(jnp.concatenate([-sin, sin], axis=1), (1, reps)))


def _dif_proj(x2d, g, tables, wq, wk, wv, *, row_tile, rows_per_table):
    n, d = x2d.shape
    hd = wq.shape[1] // (2 * DIF_HEADS)
    n_tab = rows_per_table // row_tile
    w = wq.shape[1]
    row = lambda width: pl.BlockSpec((row_tile, width), lambda i: (i, 0))
    tab = pl.BlockSpec((row_tile, V7X_LANES), lambda i: (i % n_tab, 0))
    return pl.pallas_call(
        functools.partial(_dif_proj_kernel, hd=hd),
        grid=(n // row_tile,),
        in_specs=[row(d), _const_spec((1, d)), tab, tab,
                  _const_spec(wq.shape), _const_spec(wk.shape), _const_spec(wv.shape)],
        out_specs=[row(w)] * 3,
        out_shape=[jax.ShapeDtypeStruct((n, w), BF16), jax.ShapeDtypeStruct((n, w), F32),
                   jax.ShapeDtypeStruct((n, w), F32)],
        compiler_params=_params("arbitrary"),
        name="dif_proj",
    )(x2d, g, *tables, wq, wk, wv)


def _dif_lambda(lq1, lk1, lq2, lk2):
    e1 = jnp.exp(jnp.sum(lq1 * lk1, axis=-1, keepdims=True))
    e2 = jnp.exp(jnp.sum(lq2 * lk2, axis=-1, keepdims=True))
    return e1 - e2 + DIF_LAMBDA_INIT


def _subln(o, g):
    return o * lax.rsqrt(jnp.mean(o * o, axis=-1, keepdims=True) + EPS) * g * (1.0 - DIF_LAMBDA_INIT)


def _flash_schedule(n_blocks, lead, lead_pad, block, n_pieces):
    off = lead_pad if lead else 0
    cost = {i: off + (i + 1) * block for i in range(n_blocks)}
    pieces = [[] for _ in range(n_pieces)]
    load = [0] * n_pieces
    pieces[0].append(-1)
    load[0] = 2 * block + off
    for i in sorted(cost, key=cost.get, reverse=True):
        k = load.index(min(load))
        pieces[k].append(i)
        load[k] += cost[i]
    return pieces


def _flash_work(piece, lam, q_ref, k_ref, v_ref, sg_ref, o_ref, kt_ref,
                kk_ref, vt_ref, qt_ref, pad_ref, s_ref, p_ref, *, block, lead, lead_pad, hd, n_pieces):
    t = q_ref.shape[1]
    hw = 2 * hd
    n_blocks = (t - lead) // block
    off = lead_pad if lead else 0
    lane = lax.broadcasted_iota(jnp.int32, (1, hw), 1)

    def transposed(x):
        return x.astype(F32).T.astype(BF16)

    def padded(ref, row0, n):
        pad_ref[...] = jnp.zeros(pad_ref.shape, F32)
        pad_ref[:n, :] = ref[0, row0:row0 + n, :].astype(F32)
        return pad_ref[...]

    def put(row0, n, kb, vb, qb):
        kb = kb.astype(BF16)
        z = jnp.zeros_like(kb)
        kk_ref[0, row0:row0 + n, :] = jnp.where(lane < hd, kb, z)
        kk_ref[1, row0:row0 + n, :] = jnp.where(lane >= hd, kb, z)
        vt_ref[:, row0:row0 + n] = transposed(vb)
        qt_ref[:, row0:row0 + n] = transposed(qb)

    def prepare():
        if lead:
            put(0, lead_pad, padded(k_ref, 0, lead), padded(v_ref, 0, lead), padded(q_ref, 0, lead))
        for j in range(n_blocks):
            rows = slice(lead + j * block, lead + (j + 1) * block)
            put(off + j * block, block, k_ref[0, rows, :], v_ref[0, rows, :], q_ref[0, rows, :])
        for j in range(n_blocks):
            kt_ref[0, :, j * block:(j + 1) * block] = k_ref[0, j * block:(j + 1) * block, :].T
        if lead:
            kt_ref[0, :, t - lead:] = padded(k_ref, t - lead, lead).T[:, :lead]

    def attend(q0, n_q, n_keys, out_row0, out_rows, slot):
        qt = qt_ref[:, q0:q0 + n_q]
        diag = n_keys - n_q
        key = lax.broadcasted_iota(jnp.int32, (n_q, n_q), 0)
        qry = lax.broadcasted_iota(jnp.int32, (n_q, n_q), 1)
        for c in range(2):
            s_ref[slot, c, :n_keys, :n_q] = _dot(kk_ref[c, :n_keys, :], qt)
        maps = []
        for c in range(2):
            if lead and diag > 0:
                lead_key = lax.broadcasted_iota(jnp.int32, (off, n_q), 0)
                s_ref[slot, c, :off, :n_q] = jnp.where(lead_key < lead, s_ref[slot, c, :off, :n_q], NEG_INF)
            s_ref[slot, c, diag:n_keys, :n_q] = jnp.where(key <= qry, s_ref[slot, c, diag:n_keys, :n_q], NEG_INF)
            s = s_ref[slot, c, :n_keys, :n_q]
            p = jnp.exp(s - jnp.max(s, axis=0, keepdims=True))
            denom = jnp.sum(p, axis=0, keepdims=True)
            p_ref[c, :n_keys, :n_q] = p.astype(BF16)
            maps.append(_dot(vt_ref[:, :n_keys], p_ref[c, :n_keys, :n_q]) / denom)
        a = maps[0] - lam * maps[1]
        a = a * lax.rsqrt(jnp.mean(a * a, axis=0, keepdims=True) + EPS) * sg_ref[...] * (1.0 - DIF_LAMBDA_INIT)
        o_ref[0, out_row0:out_row0 + out_rows, :] = a.T[:out_rows].astype(o_ref.dtype)

    def run(item, slot):
        if item < 0:
            prepare()
            if lead:
                attend(0, lead_pad, lead_pad, 0, lead, slot)
        else:
            attend(off + item * block, block, off + (item + 1) * block, lead + item * block, block, slot)

    for k, items in enumerate(_flash_schedule(n_blocks, lead, lead_pad, block, n_pieces)):
        @pl.when(piece == k)
        def _(items=items):
            for n, item in enumerate(items):
                run(item, n % 2)


def _flash_scratch(t, hw, block):
    lead = t % block
    lead_pad = V7X_LANES
    assert lead % V7X_BF16_ROWS == 0 and lead <= lead_pad
    t_pad = t - lead + (lead_pad if lead else 0)
    shapes = [pltpu.VMEM((2, t_pad, hw), BF16), pltpu.VMEM((hw, t_pad), BF16), pltpu.VMEM((hw, t_pad), BF16),
              pltpu.VMEM((lead_pad, hw), F32), pltpu.VMEM((2, 2, t_pad, block), F32),
              pltpu.VMEM((2, t_pad, block), BF16)]
    return shapes, dict(block=block, lead=lead, lead_pad=lead_pad)


def _decode_work(j, n_steps, lam, qrep_ref, kt_refs, v_refs, kn_ref, vn_ref, sg_ref, o_ref,
                 qbd_ref, knp_ref, vnp_ref, m_ref, l_ref, acc_ref, *, hd):
    n_rows, w = qrep_ref.shape[1:]
    hw = 2 * hd
    per_head = n_rows // DIF_HEADS
    slots = per_head // 2
    pair = 2 * per_head
    assert per_head == V7X_SUBLANES and DIF_HEADS % 2 == 0
    page = v_refs[0].shape[1] // DIF_HEADS

    @pl.when(j == 0)
    def _():
        r = lax.broadcasted_iota(jnp.int32, (n_rows, w), 0)
        chan = lax.broadcasted_iota(jnp.int32, (n_rows, w), 1)
        qbd_ref[...] = jnp.where(_div_pow2(chan, hd) == _div_pow2(r, slots), qrep_ref[0], 0.0).astype(BF16)
        m_ref[...] = jnp.full(m_ref.shape, NEG_INF, F32)
        l_ref[...] = jnp.zeros(l_ref.shape, F32)
        acc_ref[...] = jnp.zeros(acc_ref.shape, F32)

    first_of_pair = lax.broadcasted_iota(jnp.int32, (pair, hw), 0) < per_head

    def update(s, v_of_pair):
        m_prev = m_ref[...]
        m_new = jnp.maximum(m_prev, jnp.max(s, axis=1, keepdims=True))
        corr = jnp.exp(m_prev - m_new)
        p = jnp.exp(s - m_new)
        l_ref[...] = l_ref[...] * corr + jnp.sum(p, axis=1, keepdims=True)
        m_ref[...] = m_new
        for i in range(DIF_HEADS // 2):
            rows = slice(i * pair, (i + 1) * pair)
            both = _dot(p[rows].astype(BF16), v_of_pair(i))
            acc_ref[rows, :] = acc_ref[rows, :] * corr[rows] + jnp.where(first_of_pair, both[:, :hw], both[:, hw:])

    qbd = qbd_ref[...]
    s = jnp.concatenate([_dot(qbd, r[0].astype(BF16)) for r in kt_refs], axis=1)

    def cached_values(i):
        head = lambda h: jnp.concatenate(
            [r[0, pl.ds(h, page, stride=DIF_HEADS), :].astype(BF16) for r in v_refs], axis=0)
        return jnp.concatenate([head(2 * i), head(2 * i + 1)], axis=1)

    update(s, cached_values)

    @pl.when(j == n_steps - 1)
    def _():
        n_new = kn_ref.shape[1]
        knp_ref[...] = jnp.zeros(knp_ref.shape, BF16)
        vnp_ref[...] = jnp.zeros(vnp_ref.shape, BF16)
        knp_ref[:n_new, :] = kn_ref[0].astype(BF16)
        vnp_ref[:n_new, :] = vn_ref[0].astype(BF16)
        keys = knp_ref.shape[0]
        key = lax.broadcasted_iota(jnp.int32, (n_rows, keys), 1)
        tok = _mod_pow2(lax.broadcasted_iota(jnp.int32, (n_rows, keys), 0), slots)
        s_new = jnp.where(key <= tok, _dot_nt(qbd_ref[...], knp_ref[...]), NEG_INF)
        update(s_new, lambda i: vnp_ref[:, 2 * i * hw:2 * (i + 1) * hw])
        o = acc_ref[...] / l_ref[...]
        for h in range(DIF_HEADS):
            grp = o[h * per_head:(h + 1) * per_head]
            a = grp - lam * pltpu.roll(grp, slots, axis=0)
            o_ref[0, :, h * hw:(h + 1) * hw] = _subln(a, sg_ref[...])


def _attn_kernel(*refs, pages, hd, n_pieces, flash_cfg):
    qrep_ref = refs[1]
    kt_refs = refs[2:2 + pages]
    v_refs = refs[2 + pages:2 + 2 * pages]
    (kn_ref, vn_ref, lq1_ref, lk1_ref, lq2_ref, lk2_ref, sg_row_ref, sg_col_ref, q_ref, k_ref, v_ref,
     o_ref, og_ref, kt_ref,
     qbd_ref, knp_ref, vnp_ref, m_ref, l_ref, acc_ref,
     kk_ref, vt_ref, qt_ref, pad_ref, s_ref, p_ref) = refs[2 + 2 * pages:]
    j = pl.program_id(1)
    lam = _dif_lambda(lq1_ref[...], lk1_ref[...], lq2_ref[...], lk2_ref[...])
    _decode_work(j, pl.num_programs(1), lam, qrep_ref, kt_refs, v_refs, kn_ref, vn_ref, sg_row_ref, o_ref,
                 qbd_ref, knp_ref, vnp_ref, m_ref, l_ref, acc_ref, hd=hd)
    _flash_work(j % n_pieces, lam, q_ref, k_ref, v_ref, sg_col_ref, og_ref, kt_ref,
                kk_ref, vt_ref, qt_ref, pad_ref, s_ref, p_ref, hd=hd, n_pieces=n_pieces, **flash_cfg)


def _attention(qrep, cache_kt, cache_v, page_table, k_new, v_new, lams, subln_g, q, k, v, *, pages, block):
    bs, n_rows, w = qrep.shape
    b, t, _ = q.shape
    hw = w // DIF_HEADS
    hd = hw // 2
    per_head = n_rows // DIF_HEADS
    page = cache_kt.shape[2]
    n_steps = page_table.shape[1] // pages
    units_per_seq = (b * DIF_HEADS) // bs
    n_pieces = n_steps // units_per_seq
    assert page_table.shape[1] % pages == 0 and k_new.shape[1] <= V7X_LANES
    assert units_per_seq * bs == b * DIF_HEADS and n_pieces * units_per_seq == n_steps
    flash_scratch, flash_cfg = _flash_scratch(t, hw, block)

    def unit(i, j):
        u = i * units_per_seq + j // n_pieces
        return u // DIF_HEADS, u % DIF_HEADS

    kt_spec = lambda p: pl.BlockSpec((1, w, page), lambda i, j, pt: (pt[i, j * pages + p], 0, 0))
    v_spec = lambda p: pl.BlockSpec((1, page * DIF_HEADS, hw), lambda i, j, pt: (pt[i, j * pages + p], 0, 0))
    per_seq = lambda shape: pl.BlockSpec((1,) + shape, lambda i, j, pt: (i,) + (0,) * len(shape))
    const = lambda shape: pl.BlockSpec(shape, lambda i, j, pt: (0,) * len(shape))
    head = pl.BlockSpec((1, t, hw), lambda i, j, pt: (unit(i, j)[0], 0, unit(i, j)[1]))
    head_t = pl.BlockSpec((1, hw, t), lambda i, j, pt: (unit(i, j)[0], unit(i, j)[1], 0))
    grid_spec = pltpu.PrefetchScalarGridSpec(
        num_scalar_prefetch=1,
        grid=(bs, n_steps),
        in_specs=[per_seq((n_rows, w))] + [kt_spec(p) for p in range(pages)] + [v_spec(p) for p in range(pages)]
                 + [per_seq(k_new.shape[1:]), per_seq(v_new.shape[1:])]
                 + [const((1, hd))] * 4 + [const((1, hw)), const((hw, 1)), head, head, head],
        out_specs=[per_seq((per_head, w)), head, head_t],
        scratch_shapes=[pltpu.VMEM((n_rows, w), BF16), pltpu.VMEM((V7X_LANES, w), BF16),
                        pltpu.VMEM((V7X_LANES, w), BF16), pltpu.VMEM((n_rows, 1), F32),
                        pltpu.VMEM((n_rows, 1), F32), pltpu.VMEM((n_rows, hw), F32)] + flash_scratch,
    )
    return pl.pallas_call(
        functools.partial(_attn_kernel, pages=pages, hd=hd, n_pieces=n_pieces, flash_cfg=flash_cfg),
        grid_spec=grid_spec,
        out_shape=[jax.ShapeDtypeStruct((bs, per_head, w), F32), jax.ShapeDtypeStruct((b, t, w), BF16),
                   jax.ShapeDtypeStruct((b, w, t), F32)],
        compiler_params=pltpu.CompilerParams(dimension_semantics=("arbitrary", "arbitrary"),
                                             vmem_limit_bytes=V7X_ATTN_VMEM_LIMIT_BYTES),
        name="attention",
    )(page_table, qrep, *([cache_kt] * pages), *([cache_v] * pages), k_new, v_new, *lams, subln_g,
      subln_g.reshape(-1, 1), q, k, v)


def _largest_tile(n, cap):
    return max(r for r in range(V7X_BF16_ROWS, cap + 1, V7X_BF16_ROWS) if n % r == 0)


def kernel(x_prompt, x_sample, state_ret, cache_k, cache_v, page_table, state_ffn_conv, meta_tokens, norm1_g,
           norm2_g, final_g, ret_w_q, ret_w_k, ret_w_v, ret_w_g, ret_w_o, dif_w_q, dif_w_k, dif_w_v, dif_lq1,
           dif_lk1, dif_lq2, dif_lk2, dif_subln_g, dif_w_o, ffn_w_in, ffn_conv_w, ffn_conv_b, ffn_w_out):
    b, seq, d = x_prompt.shape
    db, ds, _ = x_sample.shape
    t = seq + N_META
    tp = SAMPLE_PAD_T
    assert ds <= tp
    d_ff = ffn_w_out.shape[1]
    page = cache_k.shape[1]
    past_len = page_table.shape[1] * page
    w_dif = dif_w_q.shape[1]
    hd = w_dif // (2 * DIF_HEADS)
    dk = ret_w_q.shape[1] // RET_HEADS
    bf = lambda a: a.astype(BF16)
    row = lambda a: a.reshape(1, -1).astype(F32)
    seq3 = lambda a, n, length: a.reshape(n, length, a.shape[-1])

    meta = jnp.broadcast_to(meta_tokens.astype(x_prompt.dtype)[None], (b, N_META, d))
    hp = jnp.concatenate([meta, x_prompt], axis=1).reshape(b * t, d)
    hs = jnp.pad(x_sample, ((0, 0), (0, tp - ds), (0, 0))).reshape(db * tp, d)
    pos_p = jnp.arange(t, dtype=F32)
    pos_s = jnp.tile(past_len + jnp.arange(tp, dtype=F32), db)
    tile_p = _largest_tile(t, 1024)
    tile_s = db * tp
    lams = [row(v) for v in (dif_lq1, dif_lk1, dif_lq2, dif_lk2)]
    subln_g = row(dif_subln_g)

    ret_w = [bf(w) for w in (ret_w_q, ret_w_k, ret_w_v, ret_w_g)]
    g1 = row(norm1_g[0])
    qp, kp, vp, gp = _ret_proj(hp, g1, _ret_tables(pos_p, dk), *ret_w, row_tile=tile_p, rows_per_table=t)
    qs, ks, vs, gs = _ret_proj(hs, g1, _ret_tables(pos_s, dk), *ret_w, row_tile=tile_s, rows_per_table=tile_s)
    og_p, ret_state_prompt = _retention(seq3(qp, b, t), seq3(kp, b, t), seq3(vp, b, t), seq3(gp, b, t),
                                        chunk=RET_CHUNK)
    og_s, ret_state_sample = _retention_sample(qs, ks, vs, gs, state_ret.astype(F32), n_seq=db, tp=tp, n_real=ds)

    def ffn_weights(i):
        w_a, w_g = jnp.split(ffn_w_in[i], 2, axis=-1)
        return (row(norm2_g[i]), bf(w_a), bf(w_g), ffn_conv_w[i].astype(F32), row(ffn_conv_b[i]), bf(ffn_w_out[i]))

    def prev_rows(state):
        z = jnp.zeros((db, tp, d_ff), F32)
        p1 = z.at[:, 0].set(state[:, 1])
        p2 = z.at[:, 0].set(state[:, 0]).at[:, 1].set(state[:, 1])
        return p1.reshape(db * tp, d_ff), p2.reshape(db * tp, d_ff)

    def conv_state_prompt(a_tail):
        per_seq = a_tail.reshape(b, t // tile_p, V7X_SUBLANES, d_ff)
        return per_seq[:, -1, V7X_SUBLANES - (CONV_W - 1):, :]

    def conv_state_sample(a_full):
        return a_full.reshape(db, tp, d_ff)[:, ds - (CONV_W - 1):ds, :]

    fw0 = ffn_weights(0)
    hp, a_p0 = _mix_ffn(hp, og_p.reshape(b * t, -1), bf(ret_w_o), *fw0, row_tile=tile_p, seq_len=t)
    hs, a_s0 = _mix_ffn(hs, og_s, bf(ret_w_o), *fw0, row_tile=tile_s, seq_len=tp,
                        prev=prev_rows(state_ffn_conv[0].astype(F32)), full_a=True)

    dif_w = [bf(w) for w in (dif_w_q, dif_w_k, dif_w_v)]
    g1 = row(norm1_g[1])
    q16, k32, v32 = _dif_proj(hp, g1, _dif_tables(pos_p, hd), *dif_w, row_tile=tile_p, rows_per_table=t)
    qs16, ks32, vs32 = _dif_proj(hs, g1, _dif_tables(pos_s, hd), *dif_w, row_tile=tile_s, rows_per_table=tile_s)
    ks32 = ks32.reshape(db, tp, w_dif)
    vs32 = vs32.reshape(db, tp, w_dif)
    assert ds <= DECODE_SLOTS
    q_slots = jnp.pad(qs16.reshape(db, tp, w_dif)[:, :ds].astype(F32), ((0, 0), (0, DECODE_SLOTS - ds), (0, 0)))
    qrep = jnp.tile(q_slots, (1, 2 * DIF_HEADS, 1))
    cache_kt = jnp.transpose(cache_k, (0, 2, 3, 4, 1)).reshape(-1, w_dif, page)
    og_s, og_p, kt_p = _attention(qrep, cache_kt, cache_v.reshape(-1, page * DIF_HEADS, 2 * hd), page_table,
                                  ks32, vs32, lams, subln_g, seq3(q16, b, t), seq3(k32, b, t), seq3(v32, b, t),
                                  pages=DECODE_PAGES_PER_STEP, block=ATT_BLOCK)
    og_s = jnp.pad(og_s[:, :ds], ((0, 0), (0, tp - ds), (0, 0))).astype(BF16)
    k_rows_prompt = jnp.transpose(kt_p.reshape(b, DIF_HEADS, 2, hd, t), (0, 4, 1, 2, 3))
    v_rows_prompt = v32.reshape(b, t, DIF_HEADS, 2 * hd)
    k_rows_sample = ks32[:, :ds].reshape(db, ds, DIF_HEADS, 2, hd)
    v_rows_sample = vs32[:, :ds].reshape(db, ds, DIF_HEADS, 2 * hd)

    fw1 = ffn_weights(1)
    fg = row(final_g)
    yp, a_p1 = _mix_ffn(hp, og_p.reshape(b * t, -1), bf(dif_w_o), *fw1, row_tile=tile_p, seq_len=t, final_g=fg)
    ys, a_s1 = _mix_ffn(hs, og_s.reshape(db * tp, -1), bf(dif_w_o), *fw1, row_tile=tile_s, seq_len=tp,
                        prev=prev_rows(state_ffn_conv[1].astype(F32)), final_g=fg, full_a=True)

    y_prompt = yp.reshape(b, t, d)[:, N_META:]
    y_sample = ys.reshape(db, tp, d)[:, :ds]
    conv_p = jnp.stack([conv_state_prompt(a_p0), conv_state_prompt(a_p1)])
    conv_s = jnp.stack([conv_state_sample(a_s0), conv_state_sample(a_s1)])
    return (y_prompt, y_sample, ret_state_prompt, ret_state_sample, k_rows_prompt, v_rows_prompt,
            k_rows_sample, v_rows_sample, conv_p, conv_s)
```

```python
import functools
import math

import jax
import jax.numpy as jnp
from jax import lax
from jax.experimental import pallas as pl
from jax.experimental.pallas import tpu as pltpu

F32 = jnp.float32
BF16 = jnp.bfloat16

N_META = 16
EPS = 1e-6
NEG_INF = -1e30
ROPE_THETA = 10000.0
RET_HEADS = 4
DIF_HEADS = 8
DIF_LAYER = 1
DIF_LAMBDA_INIT = 0.8 - 0.6 * math.exp(-0.3 * DIF_LAYER)
CONV_W = 3

V7X_LANES = 128
V7X_SUBLANES = 8
V7X_BF16_ROWS = 2 * V7X_SUBLANES
V7X_MXU_DIM = 256
V7X_VMEM_LIMIT_BYTES = 56 * 1024 * 1024

RET_CHUNK = V7X_MXU_DIM
ATT_BLOCK = V7X_MXU_DIM
FFN_COL_CHUNK = V7X_MXU_DIM
SAMPLE_PAD_T = V7X_BF16_ROWS
SAMPLE_WINDOW = V7X_LANES
DECODE_PAGES_PER_STEP = 16
DECODE_SLOTS = V7X_SUBLANES // 2
V7X_ATTN_VMEM_LIMIT_BYTES = 60 * 1024 * 1024


def _params(*sem):
    return pltpu.CompilerParams(dimension_semantics=sem, vmem_limit_bytes=V7X_VMEM_LIMIT_BYTES)


def _const_spec(shape):
    nd = len(shape)
    return pl.BlockSpec(shape, lambda *_: (0,) * nd, pipeline_mode=pl.Buffered(1))


def _log2(n):
    assert n > 0 and n & (n - 1) == 0, n
    return n.bit_length() - 1


def _div_pow2(x, n):
    return lax.shift_right_logical(x, _log2(n))


def _mod_pow2(x, n):
    _log2(n)
    return lax.bitwise_and(x, n - 1)


def _rmsnorm(x, g):
    return x * lax.rsqrt(jnp.mean(x * x, axis=-1, keepdims=True) + EPS) * g


def _silu(x):
    return x * (1.0 / (1.0 + jnp.exp(-x)))


def _dot(a, b):
    return jnp.dot(a, b, preferred_element_type=F32)


def _dot_nt(a, b):
    return lax.dot_general(a, b, (((1,), (1,)), ((), ())), preferred_element_type=F32)


def _dot_tn(a, b):
    return lax.dot_general(a, b, (((0,), (0,)), ((), ())), preferred_element_type=F32)


def _ret_log_decay():
    return jnp.log1p(-jnp.exp2(-5.0 - jnp.arange(RET_HEADS, dtype=F32)))


def _ret_decay_pow(n):
    return [math.exp(n * math.log1p(-2.0 ** (-5.0 - h))) for h in range(RET_HEADS)]


def _ret_proj_kernel(x_ref, g_ref, cos_ref, sa_ref, sb_ref, wq_ref, wk_ref, wv_ref, wg_ref,
                     q_ref, k_ref, v_ref, gate_ref, *, dk):
    n = _rmsnorm(x_ref[...], g_ref[...]).astype(BF16)

    def rotate_into(y, out_ref):
        for c0 in range(0, y.shape[1], V7X_LANES):
            t0 = c0 % dk
            blk = y[:, c0:c0 + V7X_LANES]
            nxt = pltpu.roll(blk, V7X_LANES - 1, axis=1)
            prv = pltpu.roll(blk, 1, axis=1)
            out = (blk * cos_ref[:, t0:t0 + V7X_LANES] + nxt * sa_ref[:, t0:t0 + V7X_LANES]
                   + prv * sb_ref[:, t0:t0 + V7X_LANES])
            out_ref[:, c0:c0 + V7X_LANES] = out.astype(out_ref.dtype)

    rotate_into(_dot(n, wq_ref[...]), q_ref)
    rotate_into(_dot(n, wk_ref[...]) * (dk ** -0.5), k_ref)
    v_ref[...] = _dot(n, wv_ref[...]).astype(v_ref.dtype)
    gate_ref[...] = _silu(_dot(n, wg_ref[...])).astype(gate_ref.dtype)


def _ret_tables(pos, dk):
    angle = 1.0 / (ROPE_THETA ** jnp.linspace(0.0, 1.0, dk // 2, dtype=F32))
    angle = jnp.repeat(angle, 2)
    th = pos[:, None] * angle[None, :]
    cos, sin = jnp.cos(th), jnp.sin(th)
    even = (jnp.arange(dk) % 2 == 0)[None, :]
    return cos, jnp.where(even, -sin, 0.0), jnp.where(even, 0.0, sin)


def _ret_proj(x2d, g, tables, wq, wk, wv, wg, *, row_tile, rows_per_table):
    n, d = x2d.shape
    dk = wq.shape[1] // RET_HEADS
    n_tab = rows_per_table // row_tile
    row = lambda w: pl.BlockSpec((row_tile, w), lambda i: (i, 0))
    tab = pl.BlockSpec((row_tile, dk), lambda i: (i % n_tab, 0))
    return pl.pallas_call(
        functools.partial(_ret_proj_kernel, dk=dk),
        grid=(n // row_tile,),
        in_specs=[row(d), _const_spec((1, d)), tab, tab, tab,
                  _const_spec(wq.shape), _const_spec(wk.shape), _const_spec(wv.shape), _const_spec(wg.shape)],
        out_specs=[row(wq.shape[1]), row(wk.shape[1]), row(wv.shape[1]), row(wg.shape[1])],
        out_shape=[jax.ShapeDtypeStruct((n, w.shape[1]), BF16) for w in (wq, wk, wv, wg)],
        compiler_params=_params("arbitrary"),
        name="ret_proj",
    )(x2d, g, *tables, wq, wk, wv, wg)


def _retention_kernel(q_ref, k_ref, v_ref, gate_ref, dec_ref, qd_ref, kd_ref, og_ref, st_ref,
                      s_ref, qp_ref, kp_ref, vp_ref, gp_ref, *, chunk, tail):
    c = pl.program_id(1)
    last = pl.num_programs(1) - 1
    dk = q_ref.shape[2] // RET_HEADS
    dv = v_ref.shape[2] // RET_HEADS

    @pl.when(c == 0)
    def _():
        s_ref[...] = jnp.zeros(s_ref.shape, F32)

    def step(q2, k2, v2, g2, length):
        g_len = _ret_decay_pow(length)
        k_fix = _ret_decay_pow(length - chunk)
        for h in range(RET_HEADS):
            qh = q2[:, h * dk:(h + 1) * dk]
            kh = k2[:, h * dk:(h + 1) * dk]
            vh = v2[:, h * dv:(h + 1) * dv]
            s = _dot_nt(qh, kh) * dec_ref[h]
            o = _dot(s.astype(BF16), vh)
            state = s_ref[h]
            o = o + _dot(qh, state.astype(BF16)) * qd_ref[h]
            k_scaled = (kh.astype(F32) * kd_ref[h]).astype(BF16)
            upd = _dot_tn(k_scaled, vh)
            if length != chunk:
                upd = upd * k_fix[h]
            s_ref[h] = g_len[h] * state + upd
            o = o * lax.rsqrt(jnp.mean(o * o, axis=-1, keepdims=True) + EPS)
            gate = g2[:, h * dv:(h + 1) * dv].astype(F32)
            og_ref[0, :length, h * dv:(h + 1) * dv] = (gate * o)[:length].astype(og_ref.dtype)

    def full_step():
        step(q_ref[0], k_ref[0], v_ref[0], gate_ref[0], chunk)

    if tail:
        pl.when(c < last)(full_step)

        @pl.when(c == last)
        def _():
            for src, dst in ((q_ref, qp_ref), (k_ref, kp_ref), (v_ref, vp_ref), (gate_ref, gp_ref)):
                dst[...] = jnp.zeros(dst.shape, dst.dtype)
                dst[:tail, :] = src[0, :tail, :]
            step(qp_ref[...], kp_ref[...], vp_ref[...], gp_ref[...], tail)
    else:
        full_step()

    @pl.when(c == last)
    def _():
        st_ref[0] = s_ref[...]


def _retention(q, k, v, gate, *, chunk):
    b, t, _ = q.shape
    dk = q.shape[2] // RET_HEADS
    dv = v.shape[2] // RET_HEADS
    n_full, tail = divmod(t, chunk)
    assert tail % V7X_BF16_ROWS == 0
    n_chunks = n_full + (1 if tail else 0)
    log_g = _ret_log_decay()
    idx = jnp.arange(chunk, dtype=F32)
    dist = idx[:, None] - idx[None, :]
    decay = jnp.where(dist >= 0, jnp.exp(log_g[:, None, None] * jnp.maximum(dist, 0.0)), 0.0)
    q_dec = jnp.broadcast_to(jnp.exp(log_g[:, None] * (idx[None, :] + 1.0))[:, :, None], (RET_HEADS, chunk, dv))
    k_dec = jnp.broadcast_to(jnp.exp(log_g[:, None] * (chunk - 1.0 - idx[None, :]))[:, :, None],
                             (RET_HEADS, chunk, dk))
    seq = lambda w: pl.BlockSpec((1, chunk, w), lambda i, c: (i, c, 0))
    return pl.pallas_call(
        functools.partial(_retention_kernel, chunk=chunk, tail=tail),
        grid=(b, n_chunks),
        in_specs=[seq(RET_HEADS * dk), seq(RET_HEADS * dk), seq(RET_HEADS * dv), seq(RET_HEADS * dv),
                  _const_spec(decay.shape), _const_spec(q_dec.shape), _const_spec(k_dec.shape)],
        out_specs=[seq(RET_HEADS * dv), pl.BlockSpec((1, RET_HEADS, dk, dv), lambda i, c: (i, 0, 0, 0))],
        out_shape=[jax.ShapeDtypeStruct((b, t, RET_HEADS * dv), BF16),
                   jax.ShapeDtypeStruct((b, RET_HEADS, dk, dv), F32)],
        scratch_shapes=[pltpu.VMEM((RET_HEADS, dk, dv), F32),
                        pltpu.VMEM((chunk, RET_HEADS * dk), BF16), pltpu.VMEM((chunk, RET_HEADS * dk), BF16),
                        pltpu.VMEM((chunk, RET_HEADS * dv), BF16), pltpu.VMEM((chunk, RET_HEADS * dv), BF16)],
        compiler_params=_params("arbitrary", "arbitrary"),
        name="retention",
    )(q, k, v, gate, decay, q_dec, k_dec)


def _retention_sample_kernel(q_ref, k_ref, v_ref, gate_ref, dec_ref, qd_ref, kd_ref, s0_ref, og_ref, st_ref,
                             *, tp, n_real):
    b = pl.program_id(0)
    dk = q_ref.shape[1] // RET_HEADS
    dv = v_ref.shape[1] // RET_HEADS
    win = SAMPLE_WINDOW
    per_win = win // tp
    w0 = pl.multiple_of(_div_pow2(b, per_win) * win, win)
    slot = _mod_pow2(b, per_win)
    r0 = pl.multiple_of(slot * tp, tp)
    q0 = pl.multiple_of(w0 + r0, tp)
    row = lax.broadcasted_iota(jnp.int32, (win, 1), 0)
    in_seq = _div_pow2(row, tp) == slot
    g_len = _ret_decay_pow(n_real)
    for h in range(RET_HEADS):
        qh = q_ref[pl.ds(q0, tp), h * dk:(h + 1) * dk]
        k_win = k_ref[pl.ds(w0, win), h * dk:(h + 1) * dk]
        v_win = v_ref[pl.ds(w0, win), h * dv:(h + 1) * dv]
        s = _dot_nt(qh, k_win) * dec_ref[h, pl.ds(r0, tp), :]
        o = _dot(s.astype(BF16), v_win)
        state = s0_ref[0, h]
        o = o + _dot(qh, state.astype(BF16)) * qd_ref[h]
        k_scaled = jnp.where(in_seq, k_win.astype(F32) * kd_ref[h], 0.0).astype(BF16)
        st_ref[0, h] = g_len[h] * state + _dot_tn(k_scaled, v_win)
        o = o * lax.rsqrt(jnp.mean(o * o, axis=-1, keepdims=True) + EPS)
        gate = gate_ref[pl.ds(q0, tp), h * dv:(h + 1) * dv].astype(F32)
        og_ref[:, h * dv:(h + 1) * dv] = (gate * o).astype(og_ref.dtype)


def _retention_sample(q, k, v, gate, s0, *, n_seq, tp, n_real):
    n = q.shape[0]
    dk = q.shape[1] // RET_HEADS
    dv = v.shape[1] // RET_HEADS
    win = SAMPLE_WINDOW
    assert n == n_seq * tp and n % win == 0 and win % tp == 0
    log_g = _ret_log_decay()
    r = jnp.arange(win)
    tok = (r % tp).astype(F32)
    same = (r[:, None] // tp) == (r[None, :] // tp)
    dist = tok[:, None] - tok[None, :]
    decay = jnp.where(same & (dist >= 0), jnp.exp(log_g[:, None, None] * jnp.maximum(dist, 0.0)), 0.0)
    q_dec = jnp.broadcast_to(jnp.exp(log_g[:, None] * (tok[None, :tp] + 1.0))[:, :, None], (RET_HEADS, tp, dv))
    k_dec = jnp.broadcast_to(jnp.exp(log_g[:, None] * (n_real - 1.0 - tok[None, :]))[:, :, None],
                             (RET_HEADS, win, dk))
    state_spec = pl.BlockSpec((1, RET_HEADS, dk, dv), lambda i: (i, 0, 0, 0))
    return pl.pallas_call(
        functools.partial(_retention_sample_kernel, tp=tp, n_real=n_real),
        grid=(n_seq,),
        in_specs=[_const_spec(q.shape), _const_spec(k.shape), _const_spec(v.shape), _const_spec(gate.shape),
                  _const_spec(decay.shape), _const_spec(q_dec.shape), _const_spec(k_dec.shape), state_spec],
        out_specs=[pl.BlockSpec((tp, RET_HEADS * dv), lambda i: (i, 0)), state_spec],
        out_shape=[jax.ShapeDtypeStruct((n, RET_HEADS * dv), BF16),
                   jax.ShapeDtypeStruct((n_seq, RET_HEADS, dk, dv), F32)],
        compiler_params=_params("arbitrary"),
        name="retention_sample",
    )(q, k, v, gate, decay, q_dec, k_dec, s0)


def _mix_ffn_kernel(*refs, seq_len, has_prev, final_norm, full_a):
    refs = list(refs)
    x_ref, og_ref, wo_ref, g2_ref, wa_ref, wg_ref, cw_ref, cb_ref, wout_ref = refs[:9]
    refs = refs[9:]
    p1_ref = p2_ref = fg_ref = None
    if has_prev:
        p1_ref, p2_ref = refs[:2]
        refs = refs[2:]
    if final_norm:
        fg_ref = refs[0]
        refs = refs[1:]
    out_ref, a_ref, carry_ref, acc_ref, ag_ref = refs
    i = pl.program_id(0)
    rows = x_ref.shape[0]
    d_ff = wa_ref.shape[1]
    keep = V7X_SUBLANES
    tile_in_sequence = seq_len % rows == 0

    @pl.when(i == 0)
    def _():
        carry_ref[...] = jnp.zeros(carry_ref.shape, F32)

    h = x_ref[...] + _dot(og_ref[...], wo_ref[...])
    n = _rmsnorm(h, g2_ref[...]).astype(BF16)
    row = lax.broadcasted_iota(jnp.int32, (rows, 1), 0)
    if tile_in_sequence:
        t = (i % (seq_len // rows)) * rows + row
    else:
        t = _mod_pow2(row, seq_len)

    def conv(am2, am1, a0, cs):
        return cw_ref[0:1, cs] * am2 + cw_ref[1:2, cs] * am1 + cw_ref[2:3, cs] * a0 + cb_ref[:, cs]

    def shifted(x, prev, row_x):
        x1 = jnp.where(row_x == 0, prev[keep - 1:keep], pltpu.roll(x, 1, axis=0))
        x2 = jnp.where(row_x == 0, prev[keep - 2:keep - 1],
                       jnp.where(row_x == 1, prev[keep - 1:keep], pltpu.roll(x, 2, axis=0)))
        return x1, x2

    acc_ref[...] = h
    for j, c0 in enumerate(range(0, d_ff, FFN_COL_CHUNK)):
        cs = slice(c0, c0 + FFN_COL_CHUNK)
        a_buf, g_buf = ag_ref.at[j % 2, 0], ag_ref.at[j % 2, 1]
        a_buf[...] = _dot(n, wa_ref[:, cs])
        g_buf[...] = _dot(n, wg_ref[:, cs])
        a, gt = a_buf[...], g_buf[...]
        prev = carry_ref[:, cs]
        if tile_in_sequence and not has_prev:
            ac = conv(pltpu.roll(a, 2, axis=0), pltpu.roll(a, 1, axis=0), a, cs)
            head1, head2 = shifted(a[:keep], prev, row[:keep])
            ac_head = conv(jnp.where(t[:keep] >= 2, head2, 0.0), jnp.where(t[:keep] >= 1, head1, 0.0),
                           a[:keep], cs)
            ac = jnp.concatenate([ac_head, ac[keep:]], axis=0)
        else:
            r1, r2 = shifted(a, prev, row)
            am1 = jnp.where(t >= 1, r1, p1_ref[:, cs] if has_prev else 0.0)
            am2 = jnp.where(t >= 2, r2, p2_ref[:, cs] if has_prev else 0.0)
            ac = conv(am2, am1, a, cs)
        u = (_silu(ac) * gt).astype(BF16)
        acc_ref[...] += _dot(u, wout_ref[cs, :])
        carry_ref[:, cs] = a[rows - keep:, :]
        if full_a:
            a_ref[:, cs] = a
        else:
            a_ref[0, :, cs] = a[rows - keep:, :]
    out = acc_ref[...]
    if final_norm:
        out = _rmsnorm(out, fg_ref[...])
    out_ref[...] = out


def _mix_ffn(x2d, og, wo, g2, wa, wg, conv_w, conv_b, wout, *, row_tile, seq_len, prev=None, final_g=None,
             full_a=False):
    n, d = x2d.shape
    d_ff = wa.shape[1]
    assert d_ff % FFN_COL_CHUNK == 0
    assert seq_len % row_tile == 0 or (row_tile % seq_len == 0 and n == row_tile)
    n_tiles = n // row_tile
    row = lambda w: pl.BlockSpec((row_tile, w), lambda i: (i, 0))
    in_specs = [row(d), row(og.shape[1]), _const_spec(wo.shape), _const_spec((1, d)), _const_spec(wa.shape),
                _const_spec(wg.shape), _const_spec(conv_w.shape), _const_spec((1, d_ff)), _const_spec(wout.shape)]
    args = [x2d, og, wo, g2, wa, wg, conv_w, conv_b, wout]
    if prev is not None:
        in_specs += [row(d_ff), row(d_ff)]
        args += list(prev)
    if final_g is not None:
        in_specs.append(_const_spec((1, d)))
        args.append(final_g)
    if full_a:
        a_spec, a_shape = row(d_ff), (n, d_ff)
    else:
        a_spec = pl.BlockSpec((1, V7X_SUBLANES, d_ff), lambda i: (i, 0, 0))
        a_shape = (n_tiles, V7X_SUBLANES, d_ff)
    return pl.pallas_call(
        functools.partial(_mix_ffn_kernel, seq_len=seq_len, has_prev=prev is not None,
                          final_norm=final_g is not None, full_a=full_a),
        grid=(n_tiles,),
        in_specs=in_specs,
        out_specs=[row(d), a_spec],
        out_shape=[jax.ShapeDtypeStruct((n, d), F32), jax.ShapeDtypeStruct(a_shape, F32)],
        scratch_shapes=[pltpu.VMEM((V7X_SUBLANES, d_ff), F32), pltpu.VMEM((row_tile, d), F32),
                        pltpu.VMEM((2, 2, row_tile, FFN_COL_CHUNK), F32)],
        compiler_params=_params("arbitrary"),
        name="mix_ffn",
    )(*args)


def _dif_proj_kernel(x_ref, g_ref, cos_ref, sin_ref, wq_ref, wk_ref, wv_ref,
                     q_ref, k_ref, v_ref, *, hd):
    n = _rmsnorm(x_ref[...], g_ref[...]).astype(BF16)
    lane = lax.broadcasted_iota(jnp.int32, (1, V7X_LANES), 1)
    first_half = _mod_pow2(lane, hd) < (hd // 2)
    cos, sin = cos_ref[...], sin_ref[...]

    def rope(y, scale, out_ref):
        for c0 in range(0, y.shape[1], V7X_LANES):
            blk = y[:, c0:c0 + V7X_LANES]
            other = jnp.where(first_half, pltpu.roll(blk, V7X_LANES - hd // 2, axis=1),
                              pltpu.roll(blk, hd // 2, axis=1))
            out = blk * cos + other * sin
            if scale != 1.0:
                out = out * scale
            out_ref[:, c0:c0 + V7X_LANES] = out.astype(out_ref.dtype)

    rope(_dot(n, wq_ref[...]), hd ** -0.5, q_ref)
    rope(_dot(n, wk_ref[...]), 1.0, k_ref)
    v_ref[...] = _dot(n, wv_ref[...])


def _dif_tables(pos, hd):
    half = hd // 2
    inv = ROPE_THETA ** (-jnp.arange(half, dtype=F32) / half)
    th = pos[:, None] * inv[None, :]
    cos, sin = jnp.cos(th), jnp.sin(th)
    reps = V7X_LANES // hd
    return (jnp.tile(jnp.concatenate([cos, cos], axis=1), (1, reps)),
            jnp.tile(jnp.concatenate([-sin, sin], axis=1), (1, reps)))


def _dif_proj(x2d, g, tables, wq, wk, wv, *, row_tile, rows_per_table):
    n, d = x2d.shape
    hd = wq.shape[1] // (2 * DIF_HEADS)
    n_tab = rows_per_table // row_tile
    w = wq.shape[1]
    row = lambda width: pl.BlockSpec((row_tile, width), lambda i: (i, 0))
    tab = pl.BlockSpec((row_tile, V7X_LANES), lambda i: (i % n_tab, 0))
    return pl.pallas_call(
        functools.partial(_dif_proj_kernel, hd=hd),
        grid=(n // row_tile,),
        in_specs=[row(d), _const_spec((1, d)), tab, tab,
                  _const_spec(wq.shape), _const_spec(wk.shape), _const_spec(wv.shape)],
        out_specs=[row(w)] * 3,
        out_shape=[jax.ShapeDtypeStruct((n, w), BF16), jax.ShapeDtypeStruct((n, w), F32),
                   jax.ShapeDtypeStruct((n, w), F32)],
        compiler_params=_params("arbitrary"),
        name="dif_proj",
    )(x2d, g, *tables, wq, wk, wv)


def _dif_lambda(lq1, lk1, lq2, lk2):
    e1 = jnp.exp(jnp.sum(lq1 * lk1, axis=-1, keepdims=True))
    e2 = jnp.exp(jnp.sum(lq2 * lk2, axis=-1, keepdims=True))
    return e1 - e2 + DIF_LAMBDA_INIT


def _subln(o, g):
    return o * lax.rsqrt(jnp.mean(o * o, axis=-1, keepdims=True) + EPS) * g * (1.0 - DIF_LAMBDA_INIT)


def _flash_schedule(n_blocks, lead, lead_pad, block, n_pieces):
    off = lead_pad if lead else 0
    cost = {i: off + (i + 1) * block for i in range(n_blocks)}
    pieces = [[] for _ in range(n_pieces)]
    load = [0] * n_pieces
    pieces[0].append(-1)
    load[0] = 2 * block + off
    for i in sorted(cost, key=cost.get, reverse=True):
        k = load.index(min(load))
        pieces[k].append(i)
        load[k] += cost[i]
    return pieces


def _flash_work(piece, lam, q_ref, k_ref, v_ref, sg_ref, o_ref, kt_ref,
                kk_ref, vt_ref, qt_ref, pad_ref, s_ref, p_ref, *, block, lead, lead_pad, hd, n_pieces,
                before=None):
    t = q_ref.shape[1]
    hw = 2 * hd
    n_blocks = (t - lead) // block
    off = lead_pad if lead else 0
    lane = lax.broadcasted_iota(jnp.int32, (1, hw), 1)

    def transposed(x):
        return x.astype(F32).T.astype(BF16)

    def padded(ref, row0, n):
        pad_ref[...] = jnp.zeros(pad_ref.shape, F32)
        pad_ref[:n, :] = ref[0, row0:row0 + n, :].astype(F32)
        return pad_ref[...]

    def put(row0, n, kb, vb, qb):
        kb = kb.astype(BF16)
        z = jnp.zeros_like(kb)
        kk_ref[0, row0:row0 + n, :] = jnp.where(lane < hd, kb, z)
        kk_ref[1, row0:row0 + n, :] = jnp.where(lane >= hd, kb, z)
        vt_ref[:, row0:row0 + n] = transposed(vb)
        qt_ref[:, row0:row0 + n] = transposed(qb)

    def prepare():
        if lead:
            put(0, lead_pad, padded(k_ref, 0, lead), padded(v_ref, 0, lead), padded(q_ref, 0, lead))
        for j in range(n_blocks):
            rows = slice(lead + j * block, lead + (j + 1) * block)
            put(off + j * block, block, k_ref[0, rows, :], v_ref[0, rows, :], q_ref[0, rows, :])
        for j in range(n_blocks):
            kt_ref[0, :, j * block:(j + 1) * block] = k_ref[0, j * block:(j + 1) * block, :].T
        if lead:
            kt_ref[0, :, t - lead:] = padded(k_ref, t - lead, lead).T[:, :lead]

    def attend(q0, n_q, n_keys, out_row0, out_rows, slot):
        qt = qt_ref[:, q0:q0 + n_q]
        diag = n_keys - n_q
        key = lax.broadcasted_iota(jnp.int32, (n_q, n_q), 0)
        qry = lax.broadcasted_iota(jnp.int32, (n_q, n_q), 1)
        for c in range(2):
            s_ref[slot, c, :n_keys, :n_q] = _dot(kk_ref[c, :n_keys, :], qt)
        maps = []
        for c in range(2):
            if lead and diag > 0:
                lead_key = lax.broadcasted_iota(jnp.int32, (off, n_q), 0)
                s_ref[slot, c, :off, :n_q] = jnp.where(lead_key < lead, s_ref[slot, c, :off, :n_q], NEG_INF)
            s_ref[slot, c, diag:n_keys, :n_q] = jnp.where(key <= qry, s_ref[slot, c, diag:n_keys, :n_q], NEG_INF)
            s = s_ref[slot, c, :n_keys, :n_q]
            p = jnp.exp(s - jnp.max(s, axis=0, keepdims=True))
            denom = jnp.sum(p, axis=0, keepdims=True)
            p_ref[c, :n_keys, :n_q] = p.astype(BF16)
            maps.append(_dot(vt_ref[:, :n_keys], p_ref[c, :n_keys, :n_q]) / denom)
        a = maps[0] - lam * maps[1]
        a = a * lax.rsqrt(jnp.mean(a * a, axis=0, keepdims=True) + EPS) * sg_ref[...] * (1.0 - DIF_LAMBDA_INIT)
        o_ref[0, out_row0:out_row0 + out_rows, :] = a.T[:out_rows].astype(o_ref.dtype)

    def run(item, slot):
        if item < 0:
            prepare()
            if lead:
                attend(0, lead_pad, lead_pad, 0, lead, slot)
        else:
            attend(off + item * block, block, off + (item + 1) * block, lead + item * block, block, slot)

    for k, items in enumerate(_flash_schedule(n_blocks, lead, lead_pad, block, n_pieces)):
        @pl.when(piece == k)
        def _(items=items):
            if before is not None:
                before()
            for n, item in enumerate(items):
                run(item, n % 2)


def _flash_scratch(t, hw, block):
    lead = t % block
    lead_pad = V7X_LANES
    assert lead % V7X_BF16_ROWS == 0 and lead <= lead_pad
    t_pad = t - lead + (lead_pad if lead else 0)
    shapes = [pltpu.VMEM((2, t_pad, hw), BF16), pltpu.VMEM((hw, t_pad), BF16), pltpu.VMEM((hw, t_pad), BF16),
              pltpu.VMEM((lead_pad, hw), F32), pltpu.VMEM((2, 2, t_pad, block), F32),
              pltpu.VMEM((2, t_pad, block), BF16)]
    return shapes, dict(block=block, lead=lead, lead_pad=lead_pad)


def _decode_work(j, n_steps, lam, qrep_ref, kt_refs, v_refs, kn_ref, vn_ref, sg_ref, o_ref,
                 qbd_ref, knp_ref, vnp_ref, m_ref, l_ref, acc_ref, *, hd):
    n_rows, w = qrep_ref.shape[1:]
    hw = 2 * hd
    per_head = n_rows // DIF_HEADS
    slots = per_head // 2
    pair = 2 * per_head
    assert per_head == V7X_SUBLANES and DIF_HEADS % 2 == 0
    page = v_refs[0].shape[1] // DIF_HEADS

    @pl.when(j == 0)
    def _():
        r = lax.broadcasted_iota(jnp.int32, (n_rows, w), 0)
        chan = lax.broadcasted_iota(jnp.int32, (n_rows, w), 1)
        qbd_ref[...] = jnp.where(_div_pow2(chan, hd) == _div_pow2(r, slots), qrep_ref[0], 0.0).astype(BF16)
        m_ref[...] = jnp.full(m_ref.shape, NEG_INF, F32)
        l_ref[...] = jnp.zeros(l_ref.shape, F32)
        acc_ref[...] = jnp.zeros(acc_ref.shape, F32)

    first_of_pair = lax.broadcasted_iota(jnp.int32, (pair, hw), 0) < per_head

    def update(s, v_of_pair):
        m_prev = m_ref[...]
        m_new = jnp.maximum(m_prev, jnp.max(s, axis=1, keepdims=True))
        corr = jnp.exp(m_prev - m_new)
        p = jnp.exp(s - m_new)
        l_ref[...] = l_ref[...] * corr + jnp.sum(p, axis=1, keepdims=True)
        m_ref[...] = m_new
        for i in range(DIF_HEADS // 2):
            rows = slice(i * pair, (i + 1) * pair)
            both = _dot(p[rows].astype(BF16), v_of_pair(i))
            acc_ref[rows, :] = acc_ref[rows, :] * corr[rows] + jnp.where(first_of_pair, both[:, :hw], both[:, hw:])

    def cached_values(i):
        head = lambda h: jnp.concatenate(
            [r[0, pl.ds(h, page, stride=DIF_HEADS), :].astype(BF16) for r in v_refs], axis=0)
        return jnp.concatenate([head(2 * i), head(2 * i + 1)], axis=1)

    def pages_step():
        qbd = qbd_ref[...]
        update(jnp.concatenate([_dot(qbd, r[0].astype(BF16)) for r in kt_refs], axis=1), cached_values)

    def finish():
        pl.when(j == n_steps - 1)(new_tokens_and_output)

    def new_tokens_and_output():
        n_new = kn_ref.shape[1]
        knp_ref[...] = jnp.zeros(knp_ref.shape, BF16)
        vnp_ref[...] = jnp.zeros(vnp_ref.shape, BF16)
        knp_ref[:n_new, :] = kn_ref[0].astype(BF16)
        vnp_ref[:n_new, :] = vn_ref[0].astype(BF16)
        keys = knp_ref.shape[0]
        key = lax.broadcasted_iota(jnp.int32, (n_rows, keys), 1)
        tok = _mod_pow2(lax.broadcasted_iota(jnp.int32, (n_rows, keys), 0), slots)
        s_new = jnp.where(key <= tok, _dot_nt(qbd_ref[...], knp_ref[...]), NEG_INF)
        update(s_new, lambda i: vnp_ref[:, 2 * i * hw:2 * (i + 1) * hw])
        o = acc_ref[...] / l_ref[...]
        for h in range(DIF_HEADS):
            grp = o[h * per_head:(h + 1) * per_head]
            a = grp - lam * pltpu.roll(grp, slots, axis=0)
            o_ref[0, :, h * hw:(h + 1) * hw] = _subln(a, sg_ref[...])

    return pages_step, finish


def _attn_kernel(*refs, pages, hd, n_pieces, flash_cfg):
    qrep_ref = refs[1]
    kt_refs = refs[2:2 + pages]
    v_refs = refs[2 + pages:2 + 2 * pages]
    (kn_ref, vn_ref, lq1_ref, lk1_ref, lq2_ref, lk2_ref, sg_row_ref, sg_col_ref, q_ref, k_ref, v_ref,
     o_ref, og_ref, kt_ref,
     qbd_ref, knp_ref, vnp_ref, m_ref, l_ref, acc_ref,
     kk_ref, vt_ref, qt_ref, pad_ref, s_ref, p_ref) = refs[2 + 2 * pages:]
    j = pl.program_id(1)
    lam = _dif_lambda(lq1_ref[...], lk1_ref[...], lq2_ref[...], lk2_ref[...])
    pages_step, finish = _decode_work(j, pl.num_programs(1), lam, qrep_ref, kt_refs, v_refs, kn_ref, vn_ref,
                                      sg_row_ref, o_ref, qbd_ref, knp_ref, vnp_ref, m_ref, l_ref, acc_ref, hd=hd)
    _flash_work(j % n_pieces, lam, q_ref, k_ref, v_ref, sg_col_ref, og_ref, kt_ref,
                kk_ref, vt_ref, qt_ref, pad_ref, s_ref, p_ref, hd=hd, n_pieces=n_pieces, before=pages_step,
                **flash_cfg)
    finish()


def _attention(qrep, cache_kt, cache_v, page_table, k_new, v_new, lams, subln_g, q, k, v, *, pages, block):
    bs, n_rows, w = qrep.shape
    b, t, _ = q.shape
    hw = w // DIF_HEADS
    hd = hw // 2
    per_head = n_rows // DIF_HEADS
    page = cache_kt.shape[2]
    n_steps = page_table.shape[1] // pages
    units_per_seq = (b * DIF_HEADS) // bs
    n_pieces = n_steps // units_per_seq
    assert page_table.shape[1] % pages == 0 and k_new.shape[1] <= V7X_LANES
    assert units_per_seq * bs == b * DIF_HEADS and n_pieces * units_per_seq == n_steps
    flash_scratch, flash_cfg = _flash_scratch(t, hw, block)

    def unit(i, j):
        u = i * units_per_seq + j // n_pieces
        return u // DIF_HEADS, u % DIF_HEADS

    kt_spec = lambda p: pl.BlockSpec((1, w, page), lambda i, j, pt: (pt[i, j * pages + p], 0, 0))
    v_spec = lambda p: pl.BlockSpec((1, page * DIF_HEADS, hw), lambda i, j, pt: (pt[i, j * pages + p], 0, 0))
    per_seq = lambda shape: pl.BlockSpec((1,) + shape, lambda i, j, pt: (i,) + (0,) * len(shape))
    const = lambda shape: pl.BlockSpec(shape, lambda i, j, pt: (0,) * len(shape))
    head = pl.BlockSpec((1, t, hw), lambda i, j, pt: (unit(i, j)[0], 0, unit(i, j)[1]))
    head_t = pl.BlockSpec((1, hw, t), lambda i, j, pt: (unit(i, j)[0], unit(i, j)[1], 0))
    grid_spec = pltpu.PrefetchScalarGridSpec(
        num_scalar_prefetch=1,
        grid=(bs, n_steps),
        in_specs=[per_seq((n_rows, w))] + [kt_spec(p) for p in range(pages)] + [v_spec(p) for p in range(pages)]
                 + [per_seq(k_new.shape[1:]), per_seq(v_new.shape[1:])]
                 + [const((1, hd))] * 4 + [const((1, hw)), const((hw, 1)), head, head, head],
        out_specs=[per_seq((per_head, w)), head, head_t],
        scratch_shapes=[pltpu.VMEM((n_rows, w), BF16), pltpu.VMEM((V7X_LANES, w), BF16),
                        pltpu.VMEM((V7X_LANES, w), BF16), pltpu.VMEM((n_rows, 1), F32),
                        pltpu.VMEM((n_rows, 1), F32), pltpu.VMEM((n_rows, hw), F32)] + flash_scratch,
    )
    return pl.pallas_call(
        functools.partial(_attn_kernel, pages=pages, hd=hd, n_pieces=n_pieces, flash_cfg=flash_cfg),
        grid_spec=grid_spec,
        out_shape=[jax.ShapeDtypeStruct((bs, per_head, w), F32), jax.ShapeDtypeStruct((b, t, w), BF16),
                   jax.ShapeDtypeStruct((b, w, t), F32)],
        compiler_params=pltpu.CompilerParams(dimension_semantics=("arbitrary", "arbitrary"),
                                             vmem_limit_bytes=V7X_ATTN_VMEM_LIMIT_BYTES),
        name="attention",
    )(page_table, qrep, *([cache_kt] * pages), *([cache_v] * pages), k_new, v_new, *lams, subln_g,
      subln_g.reshape(-1, 1), q, k, v)


def _largest_tile(n, cap):
    return max(r for r in range(V7X_BF16_ROWS, cap + 1, V7X_BF16_ROWS) if n % r == 0)


def kernel(x_prompt, x_sample, state_ret, cache_k, cache_v, page_table, state_ffn_conv, meta_tokens, norm1_g,
           norm2_g, final_g, ret_w_q, ret_w_k, ret_w_v, ret_w_g, ret_w_o, dif_w_q, dif_w_k, dif_w_v, dif_lq1,
           dif_lk1, dif_lq2, dif_lk2, dif_subln_g, dif_w_o, ffn_w_in, ffn_conv_w, ffn_conv_b, ffn_w_out):
    b, seq, d = x_prompt.shape
    db, ds, _ = x_sample.shape
    t = seq + N_META
    tp = SAMPLE_PAD_T
    assert ds <= tp
    d_ff = ffn_w_out.shape[1]
    page = cache_k.shape[1]
    past_len = page_table.shape[1] * page
    w_dif = dif_w_q.shape[1]
    hd = w_dif // (2 * DIF_HEADS)
    dk = ret_w_q.shape[1] // RET_HEADS
    bf = lambda a: a.astype(BF16)
    row = lambda a: a.reshape(1, -1).astype(F32)
    seq3 = lambda a, n, length: a.reshape(n, length, a.shape[-1])

    meta = jnp.broadcast_to(meta_tokens.astype(x_prompt.dtype)[None], (b, N_META, d))
    hp = jnp.concatenate([meta, x_prompt], axis=1).reshape(b * t, d)
    hs = jnp.pad(x_sample, ((0, 0), (0, tp - ds), (0, 0))).reshape(db * tp, d)
    pos_p = jnp.arange(t, dtype=F32)
    pos_s = jnp.tile(past_len + jnp.arange(tp, dtype=F32), db)
    tile_p = _largest_tile(t, 1024)
    tile_s = db * tp
    lams = [row(v) for v in (dif_lq1, dif_lk1, dif_lq2, dif_lk2)]
    subln_g = row(dif_subln_g)

    ret_w = [bf(w) for w in (ret_w_q, ret_w_k, ret_w_v, ret_w_g)]
    g1 = row(norm1_g[0])
    qp, kp, vp, gp = _ret_proj(hp, g1, _ret_tables(pos_p, dk), *ret_w, row_tile=tile_p, rows_per_table=t)
    qs, ks, vs, gs = _ret_proj(hs, g1, _ret_tables(pos_s, dk), *ret_w, row_tile=tile_s, rows_per_table=tile_s)
    og_p, ret_state_prompt = _retention(seq3(qp, b, t), seq3(kp, b, t), seq3(vp, b, t), seq3(gp, b, t),
                                        chunk=RET_CHUNK)
    og_s, ret_state_sample = _retention_sample(qs, ks, vs, gs, state_ret.astype(F32), n_seq=db, tp=tp, n_real=ds)

    def ffn_weights(i):
        w_a, w_g = jnp.split(ffn_w_in[i], 2, axis=-1)
        return (row(norm2_g[i]), bf(w_a), bf(w_g), ffn_conv_w[i].astype(F32), row(ffn_conv_b[i]), bf(ffn_w_out[i]))

    def prev_rows(state):
        p1 = jnp.pad(state[:, 1:], ((0, 0), (0, tp - 1), (0, 0)))
        p2 = jnp.pad(state, ((0, 0), (0, tp - (CONV_W - 1)), (0, 0)))
        return p1.reshape(db * tp, d_ff), p2.reshape(db * tp, d_ff)

    def conv_state_prompt(a_tail):
        per_seq = a_tail.reshape(b, t // tile_p, V7X_SUBLANES, d_ff)
        return per_seq[:, -1, V7X_SUBLANES - (CONV_W - 1):, :]

    def conv_state_sample(a_full):
        return a_full.reshape(db, tp, d_ff)[:, ds - (CONV_W - 1):ds, :]

    fw0 = ffn_weights(0)
    hp, a_p0 = _mix_ffn(hp, og_p.reshape(b * t, -1), bf(ret_w_o), *fw0, row_tile=tile_p, seq_len=t)
    hs, a_s0 = _mix_ffn(hs, og_s, bf(ret_w_o), *fw0, row_tile=tile_s, seq_len=tp,
                        prev=prev_rows(state_ffn_conv[0].astype(F32)), full_a=True)

    dif_w = [bf(w) for w in (dif_w_q, dif_w_k, dif_w_v)]
    g1 = row(norm1_g[1])
    q16, k32, v32 = _dif_proj(hp, g1, _dif_tables(pos_p, hd), *dif_w, row_tile=tile_p, rows_per_table=t)
    qs16, ks32, vs32 = _dif_proj(hs, g1, _dif_tables(pos_s, hd), *dif_w, row_tile=tile_s, rows_per_table=tile_s)
    ks32 = ks32.reshape(db, tp, w_dif)
    vs32 = vs32.reshape(db, tp, w_dif)
    assert ds <= DECODE_SLOTS
    q_slots = jnp.pad(qs16.reshape(db, tp, w_dif)[:, :ds].astype(F32), ((0, 0), (0, DECODE_SLOTS - ds), (0, 0)))
    qrep = jnp.tile(q_slots, (1, 2 * DIF_HEADS, 1))
    cache_kt = jnp.transpose(cache_k, (0, 2, 3, 4, 1)).reshape(-1, w_dif, page)
    og_s, og_p, kt_p = _attention(qrep, cache_kt, cache_v.reshape(-1, page * DIF_HEADS, 2 * hd), page_table,
                                  ks32, vs32, lams, subln_g, seq3(q16, b, t), seq3(k32, b, t), seq3(v32, b, t),
                                  pages=DECODE_PAGES_PER_STEP, block=ATT_BLOCK)
    og_s = jnp.pad(og_s[:, :ds], ((0, 0), (0, tp - ds), (0, 0))).astype(BF16)
    k_rows_prompt = jnp.transpose(kt_p.reshape(b, DIF_HEADS, 2, hd, t), (0, 4, 1, 2, 3))
    v_rows_prompt = v32.reshape(b, t, DIF_HEADS, 2 * hd)
    k_rows_sample = ks32[:, :ds].reshape(db, ds, DIF_HEADS, 2, hd)
    v_rows_sample = vs32[:, :ds].reshape(db, ds, DIF_HEADS, 2 * hd)

    fw1 = ffn_weights(1)
    fg = row(final_g)
    yp, a_p1 = _mix_ffn(hp, og_p.reshape(b * t, -1), bf(dif_w_o), *fw1, row_tile=tile_p, seq_len=t, final_g=fg)
    ys, a_s1 = _mix_ffn(hs, og_s.reshape(db * tp, -1), bf(dif_w_o), *fw1, row_tile=tile_s, seq_len=tp,
                        prev=prev_rows(state_ffn_conv[1].astype(F32)), final_g=fg, full_a=True)

    y_prompt = yp.reshape(b, t, d)[:, N_META:]
    y_sample = ys.reshape(db, tp, d)[:, :ds]
    conv_p = jnp.stack([conv_state_prompt(a_p0), conv_state_prompt(a_p1)])
    conv_s = jnp.stack([conv_state_sample(a_s0), conv_state_sample(a_s1)])
    return (y_prompt, y_sample, ret_state_prompt, ret_state_sample, k_rows_prompt, v_rows_prompt,
            k_rows_sample, v_rows_sample, conv_p, conv_s)
```
